```python
import math
import jax, jax.numpy as jnp
from jax import lax
import numpy as np

D_MODEL = 1024
BATCH = 8
SEQ = 4096
DEPTH = 2

HEAD_DIM = 64
ROPE_THETA = 10000.0
LN_EPS = 1e-5
DEEPNORM_ALPHA = (2.0 * DEPTH) ** 0.25
DEEPNORM_BETA = (8.0 * DEPTH) ** -0.25
N_EVEN_LAYERS = (DEPTH + 1) // 2
N_ODD_LAYERS = DEPTH // 2

NSA_HEADS = D_MODEL // 128
NSA_KV_GROUPS = 2
NSA_HEADS_PER_GROUP = NSA_HEADS // NSA_KV_GROUPS
NSA_WIDTH = NSA_HEADS * HEAD_DIM
NSA_KV_WIDTH = NSA_KV_GROUPS * HEAD_DIM
CMP_BLOCK = 32
CMP_STRIDE = 16
CMP_HIDDEN = 2 * HEAD_DIM
SEL_BLOCK = 64
SEL_TOPN = 8
WINDOW = 256
NSA_QCHUNK = 128
NSA_IN_WIDTH = NSA_WIDTH + 6 * NSA_KV_WIDTH + 3 * NSA_HEADS

S5_WIDTH = D_MODEL - NSA_WIDTH
S5_GROUP_CH = 16
S5_GROUPS = S5_WIDTH // S5_GROUP_CH
S5_STATE = 64

EVEN_IN_WIDTH = NSA_IN_WIDTH + S5_WIDTH

MOBA_HEADS = D_MODEL // HEAD_DIM
MOBA_BLOCK = 256
MOBA_TOPK = 3
MOBA_QCHUNK = 64

MOE_GROUPS = 4
MOE_EXPERTS_PER_GROUP = 8
MOE_N_EXPERTS = MOE_GROUPS * MOE_EXPERTS_PER_GROUP
MOE_TOP_IN_GROUP = 2
MOE_FF = 128

kernel_name = 'hybrid_nsa_s5_moba_hmoe_deepnorm'


def layer_norm(x, g, b):
    xf = x.astype(jnp.float32)
    mu = jnp.mean(xf, -1, keepdims=True)
    var = jnp.mean(jnp.square(xf - mu), -1, keepdims=True)
    return ((xf - mu) * lax.rsqrt(var + LN_EPS) * g + b).astype(x.dtype)


def rope_tables(seq):
    inv = 1.0 / (ROPE_THETA ** (jnp.arange(0, HEAD_DIM, 2, dtype=jnp.float32) / HEAD_DIM))
    ang = jnp.arange(seq, dtype=jnp.float32)[:, None] * inv[None, :]
    return jnp.cos(ang), jnp.sin(ang)


def apply_rope(x, cos, sin):
    x1, x2 = jnp.split(x, 2, axis=-1)
    c = cos[:, None, :]
    s = sin[:, None, :]
    return jnp.concatenate([x1 * c - x2 * s, x1 * s + x2 * c], -1).astype(x.dtype)


def masked_softmax(s, mask):
    s = jnp.where(mask, s.astype(jnp.float32), -jnp.inf)
    m = jnp.max(s, -1, keepdims=True)
    m = jnp.where(jnp.isfinite(m), m, 0.0)
    p = jnp.where(mask, jnp.exp(s - m), 0.0)
    return p / jnp.maximum(jnp.sum(p, -1, keepdims=True), 1e-30)


def nsa_mixer(xp, cos, sin, pe_k, pe_v, k_w1, k_w2, v_w1, v_w2):
    B, S, _ = xp.shape
    G, Z, dh = NSA_KV_GROUPS, NSA_HEADS_PER_GROUP, HEAD_DIM
    cuts = [int(v) for v in np.cumsum([NSA_WIDTH] + [NSA_KV_WIDTH] * 6)]
    q, kc, vc, ks, vs, kw, vw, gl = jnp.split(xp, cuts, axis=-1)
    q = apply_rope(q.reshape(B, S, NSA_HEADS, dh), cos, sin)
    q = q.reshape(B, S, G, Z, dh).transpose(0, 2, 3, 1, 4)
    kc, ks, kw = [apply_rope(t.reshape(B, S, G, dh), cos, sin) for t in (kc, ks, kw)]
    vc, vs, vw = [t.reshape(B, S, G, dh) for t in (vc, vs, vw)]
    gates = jax.nn.sigmoid(gl.astype(jnp.float32)).reshape(B, S, G, Z, 3).transpose(0, 2, 3, 1, 4)

    n_cmp = (S - CMP_BLOCK) // CMP_STRIDE + 1
    starts = jnp.arange(n_cmp) * CMP_STRIDE
    idx = starts[:, None] + jnp.arange(CMP_BLOCK)[None, :]

    def compress(t, pe, w1, w2):
        blk = t[:, idx] + pe[None, None, :, None, :]
        blk = blk.transpose(0, 3, 1, 2, 4).reshape(B, G, n_cmp, CMP_BLOCK * dh)
        return jax.nn.gelu(blk @ w1) @ w2

    k_cmp = compress(kc, pe_k, k_w1, k_w2)
    v_cmp = compress(vc, pe_v, v_w1, v_w2)
    cmp_end = starts + CMP_BLOCK - 1

    nbs = S // SEL_BLOCK
    n_sel = min(SEL_TOPN, nbs)
    bstart = jnp.arange(nbs) * SEL_BLOCK
    ovl = jnp.clip(jnp.minimum(starts[:, None] + CMP_BLOCK, bstart[None, :] + SEL_BLOCK)
                   - jnp.maximum(starts[:, None], bstart[None, :]), 0, CMP_BLOCK)
    cmp_to_sel = ovl.astype(jnp.float32) / CMP_BLOCK
    ks_blk = ks.reshape(B, nbs, SEL_BLOCK, G, dh).transpose(0, 3, 1, 2, 4)
    vs_blk = vs.reshape(B, nbs, SEL_BLOCK, G, dh).transpose(0, 3, 1, 2, 4)
    kw_pad = jnp.pad(kw.transpose(0, 2, 1, 3), ((0, 0), (0, 0), (WINDOW, 0), (0, 0)))
    vw_pad = jnp.pad(vw.transpose(0, 2, 1, 3), ((0, 0), (0, 0), (WINDOW, 0), (0, 0)))
    bi = jnp.arange(B)[:, None, None, None]
    gi = jnp.arange(G)[None, :, None, None]
    scale = dh ** -0.5
    QC = NSA_QCHUNK

    def chunk(c):
        t0 = c * QC
        qpos = t0 + jnp.arange(QC)
        qc = lax.dynamic_slice_in_dim(q, t0, QC, axis=3)
        s_c = jnp.einsum('bgzqd,bgnd->bgzqn', qc, k_cmp) * scale
        p_c = masked_softmax(s_c, cmp_end[None, :] <= qpos[:, None])
        o_c = jnp.einsum('bgzqn,bgnd->bgzqd', p_c.astype(v_cmp.dtype), v_cmp)
        imp = jnp.einsum('bgzqn,nj->bgqj', p_c, cmp_to_sel)
        qblk = qpos // SEL_BLOCK
        j = jnp.arange(nbs)
        forced = (j[None, :] == 0) | (j[None, :] == qblk[:, None]) | (j[None, :] == qblk[:, None] - 1)
        future = j[None, :] > qblk[:, None]
        score = jnp.where(future, -1.0, jnp.where(forced, 1e3, imp))
        top_s, sel = lax.top_k(score, n_sel)
        valid = top_s >= 0.0
        k_sel = ks_blk[bi, gi, sel]
        v_sel = vs_blk[bi, gi, sel]
        kpos = sel[..., None] * SEL_BLOCK + jnp.arange(SEL_BLOCK)
        m_s = valid[..., None] & (kpos <= qpos[None, None, :, None, None])
        s_s = jnp.einsum('bgzqd,bgqnkd->bgzqnk', qc, k_sel) * scale
        p_s = masked_softmax(s_s.reshape(B, G, Z, QC, n_sel * SEL_BLOCK),
                             m_s.reshape(B, G, 1, QC, n_sel * SEL_BLOCK))
        o_s = jnp.einsum('bgzqk,bgqkd->bgzqd', p_s.astype(v_sel.dtype),
                         v_sel.reshape(B, G, QC, n_sel * SEL_BLOCK, dh))
        k_w = lax.dynamic_slice_in_dim(kw_pad, t0, QC + WINDOW, axis=2)
        v_w = lax.dynamic_slice_in_dim(vw_pad, t0, QC + WINDOW, axis=2)
        kposw = t0 - WINDOW + jnp.arange(QC + WINDOW)
        diff = qpos[:, None] - kposw[None, :]
        m_w = (diff >= 0) & (diff < WINDOW) & (kposw[None, :] >= 0)
        s_w = jnp.einsum('bgzqd,bgkd->bgzqk', qc, k_w) * scale
        p_w = masked_softmax(s_w, m_w)
        o_w = jnp.einsum('bgzqk,bgkd->bgzqd', p_w.astype(v_w.dtype), v_w)
        g = lax.dynamic_slice_in_dim(gates, t0, QC, axis=3)
        o = g[..., 0:1] * o_c + g[..., 1:2] * o_s + g[..., 2:3] * o_w
        return o.astype(xp.dtype)

    outs = lax.map(chunk, jnp.arange(S // QC))
    return outs.transpose(1, 0, 4, 2, 3, 5).reshape(B, S, NSA_WIDTH)


def ssm_combine(e1, e2):
    a1r, a1i, b1r, b1i = e1
    a2r, a2i, b2r, b2i = e2
    return (a2r * a1r - a2i * a1i,
            a2r * a1i + a2i * a1r,
            a2r * b1r - a2i * b1i + b2r,
            a2r * b1i + a2i * b1r + b2i)


def s5_mixer(u, lam_re, lam_im, log_step, b_re, b_im, c_re, c_im, d_skip, glu_w, glu_b):
    B, S, _ = u.shape
    f32 = jnp.float32
    uf = u.astype(f32)
    ug = uf.reshape(B, S, S5_GROUPS, S5_GROUP_CH)
    step = jnp.exp(log_step.astype(f32))[:, None]
    lr = lam_re.astype(f32)
    li = lam_im.astype(f32)
    mag = jnp.exp(lr * step)
    ar = mag * jnp.cos(li * step)
    ai = mag * jnp.sin(li * step)
    den = lr * lr + li * li
    fr = ((ar - 1.0) * lr + ai * li) / den
    fi = (ai * lr - (ar - 1.0) * li) / den
    br = b_re.astype(f32)
    bim = b_im.astype(f32)
    bbar_r = fr[..., None] * br - fi[..., None] * bim
    bbar_i = fr[..., None] * bim + fi[..., None] * br
    bu_r = jnp.einsum('bsgc,gpc->bsgp', ug, bbar_r)
    bu_i = jnp.einsum('bsgc,gpc->bsgp', ug, bbar_i)
    a_r = jnp.broadcast_to(ar[None, None], (1, S, S5_GROUPS, S5_STATE))
    a_i = jnp.broadcast_to(ai[None, None], (1, S, S5_GROUPS, S5_STATE))
    _, _, xr, xi = lax.associative_scan(ssm_combine, (a_r, a_i, bu_r, bu_i), axis=1)
    y = (jnp.einsum('bsgp,gcp->bsgc', xr, c_re.astype(f32))
         - jnp.einsum('bsgp,gcp->bsgc', xi, c_im.astype(f32)))
    y = y.reshape(B, S, S5_WIDTH) + d_skip.astype(f32) * uf
    g = jax.nn.gelu(y)
    out = g * jax.nn.sigmoid(g @ glu_w.astype(f32) + glu_b.astype(f32))
    return out.astype(u.dtype)


def moba_mixer(xp, cos, sin):
    B, S, _ = xp.shape
    H, dh = MOBA_HEADS, HEAD_DIM
    q, k, v = jnp.split(xp, 3, axis=-1)
    q = apply_rope(q.reshape(B, S, H, dh), cos, sin).transpose(0, 2, 1, 3)
    k = apply_rope(k.reshape(B, S, H, dh), cos, sin).transpose(0, 2, 1, 3)
    v = v.reshape(B, S, H, dh).transpose(0, 2, 1, 3)
    nb = -(-S // MOBA_BLOCK)
    pad = nb * MOBA_BLOCK - S
    k = jnp.pad(k, ((0, 0), (0, 0), (0, pad), (0, 0)))
    v = jnp.pad(v, ((0, 0), (0, 0), (0, pad), (0, 0)))
    k_blk = k.reshape(B, H, nb, MOBA_BLOCK, dh)
    v_blk = v.reshape(B, H, nb, MOBA_BLOCK, dh)
    k_mean = jnp.mean(k_blk.astype(jnp.float32), axis=3)
    top = min(MOBA_TOPK, max(nb - 1, 1))
    bi = jnp.arange(B)[:, None, None, None]
    hi = jnp.arange(H)[None, :, None, None]
    scale = dh ** -0.5
    QC = MOBA_QCHUNK

    def chunk(c):
        t0 = c * QC
        qpos = t0 + jnp.arange(QC)
        qc = lax.dynamic_slice_in_dim(q, t0, QC, axis=2)
        own = t0 // MOBA_BLOCK
        gs = jnp.einsum('bhqd,bhnd->bhqn', qc.astype(jnp.float32), k_mean)
        past = jnp.arange(nb) < own
        gs = jnp.where(past, gs, -jnp.inf)
        top_s, sel = lax.top_k(gs, top)
        valid = jnp.isfinite(top_s)
        k_sel = k_blk[bi, hi, sel]
        v_sel = v_blk[bi, hi, sel]
        s_sel = jnp.einsum('bhqd,bhqnkd->bhqnk', qc, k_sel) * scale
        s_sel = s_sel.reshape(B, H, QC, top * MOBA_BLOCK)
        m_sel = jnp.broadcast_to(valid[..., None], (B, H, QC, top, MOBA_BLOCK)).reshape(B, H, QC, top * MOBA_BLOCK)
        k_own = lax.dynamic_slice_in_dim(k, own * MOBA_BLOCK, MOBA_BLOCK, axis=2)
        v_own = lax.dynamic_slice_in_dim(v, own * MOBA_BLOCK, MOBA_BLOCK, axis=2)
        kpos_own = own * MOBA_BLOCK + jnp.arange(MOBA_BLOCK)
        m_own = jnp.broadcast_to(kpos_own[None, :] <= qpos[:, None], (B, H, QC, MOBA_BLOCK))
        s_own = jnp.einsum('bhqd,bhkd->bhqk', qc, k_own) * scale
        p = masked_softmax(jnp.concatenate([s_own, s_sel], -1), jnp.concatenate([m_own, m_sel], -1))
        p = p.astype(v.dtype)
        o = (jnp.einsum('bhqk,bhkd->bhqd', p[..., :MOBA_BLOCK], v_own)
             + jnp.einsum('bhqnk,bhqnkd->bhqd', p[..., MOBA_BLOCK:].reshape(B, H, QC, top, MOBA_BLOCK), v_sel))
        return o

    outs = lax.map(chunk, jnp.arange(S // QC))
    return outs.transpose(1, 0, 3, 2, 4).reshape(B, S, H * dh)


def hier_moe(x, w_coarse, b_coarse, w_fine, b_fine, w_gate, w_up, w_down):
    B, S, D = x.shape
    t = x.reshape(B * S, D)
    T = t.shape[0]
    pc = jax.nn.softmax((t @ w_coarse + b_coarse).astype(jnp.float32), axis=-1)
    gv, gidx = lax.top_k(pc, 1)
    lf = (jnp.einsum('td,gde->tge', t, w_fine) + b_fine).astype(jnp.float32)
    lf_sel = jnp.take_along_axis(lf, gidx[:, :, None], axis=1)[:, 0]
    pf = jax.nn.softmax(lf_sel, axis=-1)
    fv, fidx = lax.top_k(pf, MOE_TOP_IN_GROUP)
    fv = fv / jnp.sum(fv, -1, keepdims=True)
    w = gv * fv
    eid = gidx * MOE_EXPERTS_PER_GROUP + fidx
    gates = jnp.sum(jax.nn.one_hot(eid, MOE_N_EXPERTS, dtype=jnp.float32) * w[..., None], axis=1)
    gates = gates.reshape(T, MOE_GROUPS, MOE_EXPERTS_PER_GROUP).astype(t.dtype)
    y = jnp.zeros_like(t)
    for g in range(MOE_GROUPS):
        hg = jnp.einsum('td,edf->tef', t, w_gate[g])
        hu = jnp.einsum('td,edf->tef', t, w_up[g])
        h = jax.nn.silu(hg) * hu * gates[:, g, :, None]
        y = y + jnp.einsum('tef,efd->td', h, w_down[g])
    return y.reshape(B, S, D)


def setup_inputs(seed: int = 0) -> dict:
    key = jax.random.key(seed)
    ks = jax.random.split(key, 40)
    f32 = jnp.float32
    NE, NO, L = N_EVEN_LAYERS, N_ODD_LAYERS, DEPTH

    def nrm(i, shape, scale):
        return jax.random.normal(ks[i], shape, f32) * scale

    d = D_MODEL
    return {
        'x': nrm(0, (BATCH, SEQ, d), 1.0),
        'ev_w_in': nrm(1, (NE, d, EVEN_IN_WIDTH), d ** -0.5),
        'nsa_pe_k': nrm(2, (NE, CMP_BLOCK, HEAD_DIM), 0.02),
        'nsa_pe_v': nrm(3, (NE, CMP_BLOCK, HEAD_DIM), 0.02),
        'nsa_cmp_k_w1': nrm(4, (NE, CMP_BLOCK * HEAD_DIM, CMP_HIDDEN), (CMP_BLOCK * HEAD_DIM) ** -0.5),
        'nsa_cmp_k_w2': nrm(5, (NE, CMP_HIDDEN, HEAD_DIM), CMP_HIDDEN ** -0.5),
        'nsa_cmp_v_w1': nrm(6, (NE, CMP_BLOCK * HEAD_DIM, CMP_HIDDEN), (CMP_BLOCK * HEAD_DIM) ** -0.5),
        'nsa_cmp_v_w2': nrm(7, (NE, CMP_HIDDEN, HEAD_DIM), CMP_HIDDEN ** -0.5),
        's5_lambda_re': -0.5 + nrm(8, (NE, S5_GROUPS, S5_STATE), 0.01),
        's5_lambda_im': math.pi * jnp.arange(S5_STATE, dtype=f32) + nrm(9, (NE, S5_GROUPS, S5_STATE), 0.01),
        's5_log_step': jax.random.uniform(ks[10], (NE, S5_GROUPS), f32, math.log(1e-3), math.log(1e-1)),
        's5_b_re': nrm(11, (NE, S5_GROUPS, S5_STATE, S5_GROUP_CH), (2 * S5_GROUP_CH) ** -0.5),
        's5_b_im': nrm(12, (NE, S5_GROUPS, S5_STATE, S5_GROUP_CH), (2 * S5_GROUP_CH) ** -0.5),
        's5_c_re': nrm(13, (NE, S5_GROUPS, S5_GROUP_CH, S5_STATE), S5_STATE ** -0.5),
        's5_c_im': nrm(14, (NE, S5_GROUPS, S5_GROUP_CH, S5_STATE), S5_STATE ** -0.5),
        's5_d': nrm(15, (NE, S5_WIDTH), 1.0),
        's5_glu_w': nrm(16, (NE, S5_WIDTH, S5_WIDTH), S5_WIDTH ** -0.5),
        's5_glu_b': nrm(17, (NE, S5_WIDTH), 0.01),
        'ev_w_out': nrm(18, (NE, d, d), DEEPNORM_BETA * d ** -0.5),
        'od_w_in': nrm(19, (NO, d, 3 * d), d ** -0.5),
        'od_w_out': nrm(20, (NO, d, d), DEEPNORM_BETA * d ** -0.5),
        'ln_mix_g': 1.0 + nrm(21, (L, d), 0.01),
        'ln_mix_b': nrm(22, (L, d), 0.01),
        'ln_ffn_g': 1.0 + nrm(23, (L, d), 0.01),
        'ln_ffn_b': nrm(24, (L, d), 0.01),
        'moe_w_coarse': nrm(25, (L, d, MOE_GROUPS), d ** -0.5),
        'moe_b_coarse': nrm(26, (L, MOE_GROUPS), 0.01),
        'moe_w_fine': nrm(27, (L, MOE_GROUPS, d, MOE_EXPERTS_PER_GROUP), d ** -0.5),
        'moe_b_fine': nrm(28, (L, MOE_GROUPS, MOE_EXPERTS_PER_GROUP), 0.01),
        'moe_w_gate': nrm(29, (L, MOE_GROUPS, MOE_EXPERTS_PER_GROUP, d, MOE_FF), d ** -0.5),
        'moe_w_up': nrm(30, (L, MOE_GROUPS, MOE_EXPERTS_PER_GROUP, d, MOE_FF), d ** -0.5),
        'moe_w_down': nrm(31, (L, MOE_GROUPS, MOE_EXPERTS_PER_GROUP, MOE_FF, d), DEEPNORM_BETA * MOE_FF ** -0.5),
    }


def reference(x, ev_w_in, nsa_pe_k, nsa_pe_v, nsa_cmp_k_w1, nsa_cmp_k_w2, nsa_cmp_v_w1, nsa_cmp_v_w2,
              s5_lambda_re, s5_lambda_im, s5_log_step, s5_b_re, s5_b_im, s5_c_re, s5_c_im, s5_d,
              s5_glu_w, s5_glu_b, ev_w_out, od_w_in, od_w_out,
              ln_mix_g, ln_mix_b, ln_ffn_g, ln_ffn_b,
              moe_w_coarse, moe_b_coarse, moe_w_fine, moe_b_fine, moe_w_gate, moe_w_up, moe_w_down):
    S = x.shape[1]
    cos, sin = rope_tables(S)
    for layer in range(DEPTH):
        if layer % 2 == 0:
            e = layer // 2
            xp = x @ ev_w_in[e]
            o_nsa = nsa_mixer(xp[..., :NSA_IN_WIDTH], cos, sin, nsa_pe_k[e], nsa_pe_v[e],
                              nsa_cmp_k_w1[e], nsa_cmp_k_w2[e], nsa_cmp_v_w1[e], nsa_cmp_v_w2[e])
            o_s5 = s5_mixer(xp[..., NSA_IN_WIDTH:], s5_lambda_re[e], s5_lambda_im[e], s5_log_step[e],
                            s5_b_re[e], s5_b_im[e], s5_c_re[e], s5_c_im[e], s5_d[e], s5_glu_w[e], s5_glu_b[e])
            mix = jnp.concatenate([o_nsa, o_s5], axis=-1) @ ev_w_out[e]
        else:
            o = layer // 2
            mix = moba_mixer(x @ od_w_in[o], cos, sin) @ od_w_out[o]
        x = layer_norm(DEEPNORM_ALPHA * x + mix, ln_mix_g[layer], ln_mix_b[layer])
        ffn = hier_moe(x, moe_w_coarse[layer], moe_b_coarse[layer], moe_w_fine[layer], moe_b_fine[layer],
                       moe_w_gate[layer], moe_w_up[layer], moe_w_down[layer])
        x = layer_norm(DEEPNORM_ALPHA * x + ffn, ln_ffn_g[layer], ln_ffn_b[layer])
    return x
```

```python
import functools
import math

import jax
import jax.numpy as jnp
from jax import lax
from jax.experimental import pallas as pl
from jax.experimental.pallas import tpu as pltpu

F32 = jnp.float32
BF16 = jnp.bfloat16
I32 = jnp.int32

HEAD_DIM = 64
ROPE_THETA = 10000.0
LN_EPS = 1e-5

NSA_KV_GROUPS = 2
NSA_HEADS_PER_GROUP = 4
NSA_HEADS = NSA_KV_GROUPS * NSA_HEADS_PER_GROUP
NSA_WIDTH = NSA_HEADS * HEAD_DIM
NSA_KV_WIDTH = NSA_KV_GROUPS * HEAD_DIM
CMP_BLOCK = 32
CMP_STRIDE = 16
SEL_BLOCK = 64
SEL_TOPN = 8
WINDOW = 256
NSA_QCHUNK = 128
NSA_KV_TILE = 512

S5_GROUP_CH = 16
S5_STATE = 64
S5_CHUNK = 64

MOBA_BLOCK = 256
MOBA_TOPK = 3

MOE_GROUPS = 4
MOE_EXPERTS_PER_GROUP = 8
MOE_FF = 128

LANES = 128
ROW_TILE = 512
VMEM_LIMIT = 56 * 1024 * 1024
NEG = -1e30


def _cparams(*sem):
    return pltpu.CompilerParams(dimension_semantics=sem, vmem_limit_bytes=VMEM_LIMIT)


def _dot(a, b):
    return jnp.dot(a, b, preferred_element_type=F32)


def _dot_nt(a, b):
    return lax.dot_general(a, b, (((1,), (1,)), ((), ())), preferred_element_type=F32)


def _split(a):
    hi = a.astype(BF16)
    lo = (a - hi.astype(F32)).astype(BF16)
    return hi, lo


def _dot3(a, b):
    ah, al = _split(a)
    bh, bl = _split(b)
    return _dot(ah, bh) + _dot(ah, bl) + _dot(al, bh)


def _dot3_nt(a, b):
    ah, al = _split(a)
    bh, bl = _split(b)
    return _dot_nt(ah, bh) + _dot_nt(ah, bl) + _dot_nt(al, bh)


def _sigmoid(x):
    return 1.0 / (1.0 + jnp.exp(-x))


def _gelu_tanh(x):
    return 0.5 * x * (1.0 + jnp.tanh(math.sqrt(2.0 / math.pi) * (x + 0.044715 * (x * x * x))))


def _layer_norm(r, g, b):
    mu = jnp.mean(r, axis=-1, keepdims=True)
    d = r - mu
    var = jnp.mean(d * d, axis=-1, keepdims=True)
    return d * lax.rsqrt(var + LN_EPS) * g + b


def _masked_softmax(s, ok):
    sm = jnp.where(ok, s, NEG)
    m = jnp.max(sm, axis=-1, keepdims=True)
    p = jnp.where(ok, jnp.exp(sm - m), 0.0)
    return p / jnp.maximum(jnp.sum(p, axis=-1, keepdims=True), 1e-30)


def _rope_128(a, cos2, sin2):
    lane = lax.broadcasted_iota(I32, a.shape, 1)
    first_half = (lane & (HEAD_DIM - 1)) < (HEAD_DIM // 2)
    partner = jnp.where(first_half, pltpu.roll(a, LANES - HEAD_DIM // 2, 1), pltpu.roll(a, HEAD_DIM // 2, 1))
    return a * cos2 + partner * sin2


def _proj_kernel(x_ref, cos_ref, sin_ref, *refs, kinds):
    n = len(kinds)
    w_refs, o_refs = refs[:n], refs[n:]
    xb = x_ref[...].astype(BF16)
    cos2 = cos_ref[...]
    sin2 = sin_ref[...]
    for kind, w_ref, o_ref in zip(kinds, w_refs, o_refs):
        width = w_ref.shape[1]
        for c0 in range(0, width, LANES):
            acc = _dot(xb, w_ref[:, c0:c0 + LANES])
            if kind == "rope":
                acc = _rope_128(acc, cos2, sin2)
            elif kind == "sigmoid":
                acc = _sigmoid(acc)
            o_ref[:, c0:c0 + LANES] = acc.astype(o_ref.dtype)


def _project(x2d, cos2, sin2, weights, kinds, dtypes, seq):
    T, D = x2d.shape
    tm = min(ROW_TILE, seq)
    pos_tiles = seq // tm
    in_specs = [pl.BlockSpec((tm, D), lambda i: (i, 0)),
                pl.BlockSpec((tm, LANES), lambda i: (i % pos_tiles, 0)),
                pl.BlockSpec((tm, LANES), lambda i: (i % pos_tiles, 0))]
    in_specs += [pl.BlockSpec(w.shape, lambda i: (0, 0)) for w in weights]
    out_specs = [pl.BlockSpec((tm, w.shape[1]), lambda i: (i, 0)) for w in weights]
    out_shape = [jax.ShapeDtypeStruct((T, w.shape[1]), dt) for w, dt in zip(weights, dtypes)]
    return pl.pallas_call(
        functools.partial(_proj_kernel, kinds=tuple(kinds)),
        grid=(T // tm,), in_specs=in_specs, out_specs=out_specs, out_shape=out_shape,
        compiler_params=_cparams("parallel"), name="proj",
    )(x2d, cos2, sin2, *weights)


def _compress_kernel(r_ref, pe_ref, w1_ref, w2_ref, w2t_ref, o_ref, ot_ref):
    half = CMP_STRIDE * HEAD_DIM
    r = r_ref[0, 0]
    n = r.shape[0]
    top = _dot(r, w1_ref[0, :half, :])
    bot = _dot(r, w1_ref[0, half:, :])
    bias = _dot(pe_ref[0].astype(BF16), w1_ref[0])
    h = top + pltpu.roll(bot, n - 1, 0) + bias
    hb = _gelu_tanh(h).astype(BF16)
    o_ref[0, 0] = _dot(hb, w2_ref[0]).astype(o_ref.dtype)
    ot_ref[0, 0] = _dot_nt(w2t_ref[0], hb).astype(ot_ref.dtype)


def _nsa_compress(r, pe, w1, w2, w2t):
    _, bg, n, width = r.shape
    hid = w1.shape[2]
    return pl.pallas_call(
        _compress_kernel,
        grid=(2, bg),
        in_specs=[pl.BlockSpec((1, 1, n, width), lambda a, i: (a, i, 0, 0)),
                  pl.BlockSpec((1, 1, 2 * width), lambda a, i: (a, 0, 0)),
                  pl.BlockSpec((1, 2 * width, hid), lambda a, i: (a, 0, 0)),
                  pl.BlockSpec((1, hid, HEAD_DIM), lambda a, i: (a, 0, 0)),
                  pl.BlockSpec((1, HEAD_DIM, hid), lambda a, i: (a, 0, 0))],
        out_specs=[pl.BlockSpec((1, 1, n, HEAD_DIM), lambda a, i: (a, i, 0, 0)),
                   pl.BlockSpec((1, 1, HEAD_DIM, n), lambda a, i: (a, i, 0, 0))],
        out_shape=[jax.ShapeDtypeStruct((2, bg, n, HEAD_DIM), BF16),
                   jax.ShapeDtypeStruct((2, bg, HEAD_DIM, n), BF16)],
        compiler_params=_cparams("parallel", "parallel"), name="nsa_compress",
    )(r, pe, w1, w2, w2t)


def _nsa_attn_kernel(q_ref, kct_ref, vc_ref, kst_ref, vs_ref, kwt_ref, vw_ref, g_ref, c2s_ref, exp_ref, o_ref,
                     *, seq, n_sel):
    Z, dh, QC, KT = NSA_HEADS_PER_GROUP, HEAD_DIM, NSA_QCHUNK, NSA_KV_TILE
    ncp = seq // CMP_STRIDE
    nbs = seq // SEL_BLOCK
    c = pl.program_id(2)
    t0 = c * QC
    q4 = (q_ref[0, 0] * jnp.asarray(dh ** -0.5, BF16)).reshape(Z * QC, dh)
    qpos = t0 + lax.broadcasted_iota(I32, (QC, 1), 0)

    s_c = _dot(q4, kct_ref[0, 0]).reshape(Z, QC, ncp)
    cmp_end = lax.broadcasted_iota(I32, (1, ncp), 1) * CMP_STRIDE + (CMP_BLOCK - 1)
    p_c = _masked_softmax(s_c, (cmp_end <= qpos)[None])
    o_c = _dot(p_c.reshape(Z * QC, ncp).astype(BF16), vc_ref[0, 0])

    p_sum = p_c[0]
    for z in range(1, Z):
        p_sum = p_sum + p_c[z]
    p_hi, p_lo = _split(p_sum)
    imp = _dot(p_hi, c2s_ref[...]) + _dot(p_lo, c2s_ref[...])

    jf = lax.broadcasted_iota(I32, (QC, nbs), 1).astype(F32)
    qblk = (qpos // SEL_BLOCK).astype(F32)
    future = jf > qblk
    forced = (jf == 0.0) | (jf == qblk) | (jf == qblk - 1.0)
    score = jnp.where(future, -1.0, jnp.where(forced, 1e3, imp))
    selm = jnp.zeros((QC, nbs), F32)
    for _ in range(n_sel):
        m = jnp.max(score, axis=-1, keepdims=True)
        idx = jnp.min(jnp.where(score == m, jf, float(nbs)), axis=-1, keepdims=True)
        hit = jf == idx
        selm = jnp.where(hit, jnp.where(m >= 0.0, 1.0, 0.0), selm)
        score = jnp.where(hit, NEG, score)
    selm_b = selm.astype(BF16)

    def sel_step(kt, carry):
        m_i, l_i, acc = carry
        k0 = pl.multiple_of(kt * KT, KT)
        s = _dot(q4, kst_ref[0, 0, :, pl.ds(k0, KT)]).reshape(Z, QC, KT)
        picked = _dot(selm_b, exp_ref[:, pl.ds(k0, KT)])
        kpos = k0 + lax.broadcasted_iota(I32, (1, KT), 1)
        ok = jnp.where(kpos <= qpos, picked, 0.0) > 0.5
        sm = jnp.where(ok[None], s, NEG)
        m_new = jnp.maximum(m_i, jnp.max(sm, axis=-1, keepdims=True))
        p = jnp.where(ok[None], jnp.exp(sm - m_new), 0.0)
        alpha = jnp.exp(m_i - m_new)
        l_new = alpha * l_i + jnp.sum(p, axis=-1, keepdims=True)
        pv = _dot(p.reshape(Z * QC, KT).astype(BF16), vs_ref[0, 0, pl.ds(k0, KT), :])
        return m_new, l_new, alpha * acc + pv.reshape(Z, QC, dh)

    n_tiles = (t0 + QC + KT - 1) // KT
    init = (jnp.full((Z, QC, 1), NEG, F32), jnp.zeros((Z, QC, 1), F32), jnp.zeros((Z, QC, dh), F32))
    _, l_s, acc_s = lax.fori_loop(0, n_tiles, sel_step, init)
    o_s = acc_s / jnp.maximum(l_s, 1e-30)

    band = QC + WINDOW
    w0 = pl.multiple_of(jnp.maximum(t0 - WINDOW, 0), QC)
    s_w = _dot(q4, kwt_ref[0, 0, :, pl.ds(w0, band)]).reshape(Z, QC, band)
    dist = qpos - (w0 + lax.broadcasted_iota(I32, (1, band), 1))
    ok_w = jnp.where(dist >= 0, dist, WINDOW) < WINDOW
    p_w = _masked_softmax(s_w, ok_w[None])
    o_w = _dot(p_w.reshape(Z * QC, band).astype(BF16), vw_ref[0, 0, pl.ds(w0, band), :])

    o_c = o_c.reshape(Z, QC, dh)
    o_w = o_w.reshape(Z, QC, dh)
    g = g_ref[0, 0]
    heads = []
    for z in range(Z):
        heads.append(g[:, 3 * z:3 * z + 1] * o_c[z] + g[:, 3 * z + 1:3 * z + 2] * o_s[z]
                     + g[:, 3 * z + 2:3 * z + 3] * o_w[z])
    o_ref[0] = jnp.concatenate(heads, axis=-1).astype(o_ref.dtype)


def _nsa_attention(q, kct, vc, kst, vs, kwt, vw, gates, c2s, expand, seq):
    B, G, Z, _, dh = q.shape
    QC = NSA_QCHUNK
    ncp = seq // CMP_STRIDE
    nbs = seq // SEL_BLOCK
    full = lambda shp: pl.BlockSpec((1, 1) + shp, lambda b, g, c: (b, g, 0, 0))
    return pl.pallas_call(
        functools.partial(_nsa_attn_kernel, seq=seq, n_sel=min(SEL_TOPN, nbs)),
        grid=(B, G, seq // QC),
        in_specs=[pl.BlockSpec((1, 1, Z, QC, dh), lambda b, g, c: (b, g, 0, c, 0)),
                  full((dh, ncp)), full((ncp, dh)), full((dh, seq)), full((seq, dh)), full((dh, seq)), full((seq, dh)),
                  pl.BlockSpec((1, 1, QC, 3 * Z), lambda b, g, c: (b, g, c, 0)),
                  pl.BlockSpec((ncp, nbs), lambda b, g, c: (0, 0)),
                  pl.BlockSpec((nbs, seq), lambda b, g, c: (0, 0))],
        out_specs=pl.BlockSpec((1, QC, Z * dh), lambda b, g, c: (b, c, g)),
        out_shape=jax.ShapeDtypeStruct((B, seq, G * Z * dh), BF16),
        compiler_params=_cparams("parallel", "parallel", "arbitrary"), name="nsa_attn",
    )(q, kct, vc, kst, vs, kwt, vw, gates, c2s, expand)


def _cmul(ar, ai, br, bi):
    return ar * br - ai * bi, ar * bi + ai * br


def _s5_ops_kernel(lre_ref, lim_ref, lstep_ref, bret_ref, bimt_ref, cre_ref, cim_ref,
                   mt_ref, pm_ref, qt_ref, al_ref):
    L, C, P = S5_CHUNK, S5_GROUP_CH, S5_STATE
    lr = lre_ref[0]
    li = lim_ref[0]
    step = jnp.exp(lstep_ref[0])
    mag = jnp.exp(lr * step)
    ar = mag * jnp.cos(li * step)
    ai = mag * jnp.sin(li * step)
    den = lr * lr + li * li
    fr = ((ar - 1.0) * lr + ai * li) / den
    fi = (ai * lr - (ar - 1.0) * li) / den
    bbr, bbi = _cmul(fr, fi, bret_ref[0], bimt_ref[0])

    up_r, up_i = jnp.ones((1, 1, P), F32), jnp.zeros((1, 1, P), F32)
    dn_r, dn_i = up_r, up_i
    sr, si = ar.reshape(1, 1, P), ai.reshape(1, 1, P)
    a1r, a1i = sr, si
    n = 1
    while n < L:
        tr, ti = _cmul(up_r, up_i, sr, si)
        up_r, up_i = jnp.concatenate([up_r, tr], 0), jnp.concatenate([up_i, ti], 0)
        tr, ti = _cmul(dn_r, dn_i, sr, si)
        dn_r, dn_i = jnp.concatenate([tr, dn_r], 0), jnp.concatenate([ti, dn_i], 0)
        sr, si = _cmul(sr, si, sr, si)
        n *= 2
    al_ref[0] = jnp.concatenate([sr.reshape(1, P), si.reshape(1, P)], -1)

    cr = cre_ref[0][None]
    ci = cim_ref[0][None]
    nr, ni = _cmul(up_r, up_i, a1r, a1i)
    wr, wi = _cmul(cr, ci, nr, ni)
    qt_ref[0] = jnp.concatenate([wr.reshape(L * C, P), -wi.reshape(L * C, P)], -1)
    er, ei = _cmul(dn_r, dn_i, bbr[None], bbi[None])
    pm_ref[0] = jnp.concatenate([er.reshape(L * C, P), ei.reshape(L * C, P)], -1).astype(pm_ref.dtype)
    w0r, w0i = _cmul(cr, ci, up_r, up_i)
    kt = _dot3_nt(bbr, w0r.reshape(L * C, P)) - _dot3_nt(bbi, w0i.reshape(L * C, P))
    lane = lax.broadcasted_iota(I32, kt.shape, 1)
    mt_ref[0, 0:C, :] = kt.astype(mt_ref.dtype)
    for s in range(1, L):
        shifted = jnp.where(lane >= s * C, pltpu.roll(kt, s * C, 1), 0.0)
        mt_ref[0, s * C:(s + 1) * C, :] = shifted.astype(mt_ref.dtype)


def _s5_operators(lre, lim, lstep, bret, bimt, cre, cim):
    G = lre.shape[0]
    L, C, P = S5_CHUNK, S5_GROUP_CH, S5_STATE
    vec = pl.BlockSpec((1, 1, P), lambda g: (g, 0, 0))
    mat = pl.BlockSpec((1, C, P), lambda g: (g, 0, 0))
    return pl.pallas_call(
        _s5_ops_kernel,
        grid=(G,),
        in_specs=[vec, vec, pl.BlockSpec((1, 1, 1), lambda g: (g, 0, 0)), mat, mat, mat, mat],
        out_specs=[pl.BlockSpec((1, L * C, L * C), lambda g: (g, 0, 0)),
                   pl.BlockSpec((1, L * C, 2 * P), lambda g: (g, 0, 0)),
                   pl.BlockSpec((1, L * C, 2 * P), lambda g: (g, 0, 0)),
                   pl.BlockSpec((1, 1, 2 * P), lambda g: (g, 0, 0))],
        out_shape=[jax.ShapeDtypeStruct((G, L * C, L * C), BF16),
                   jax.ShapeDtypeStruct((G, L * C, 2 * P), BF16),
                   jax.ShapeDtypeStruct((G, L * C, 2 * P), F32),
                   jax.ShapeDtypeStruct((G, 1, 2 * P), F32)],
        compiler_params=_cparams("parallel"), name="s5_operators",
    )(lre, lim, lstep, bret, bimt, cre, cim)


def _s5_scan_kernel(u_ref, mt_ref, pm_ref, qt_ref, al_ref, d_ref, y_ref, xin_ref, *, batch):
    P = S5_STATE
    rows = u_ref.shape[1]
    u = u_ref[0]
    ub = u.astype(BF16)
    y_ref[0] = _dot(ub, mt_ref[0]) + d_ref[0] * u
    xloc = _dot(ub, pm_ref[0])
    al = al_ref[0]
    lane = lax.broadcasted_iota(I32, (1, 2 * P), 1)
    mul_same = jnp.where(lane < P, al, pltpu.roll(al, P, 1))
    mul_swap = jnp.where(lane < P, -pltpu.roll(al, P, 1), al)
    xin_ref[...] = xloc

    def carry_step(k, state):
        r0 = pl.multiple_of(k * batch, batch)
        loc = xin_ref[pl.ds(r0, batch), :]
        xin_ref[pl.ds(r0, batch), :] = state
        return state * mul_same + pltpu.roll(state, P, 1) * mul_swap + loc

    lax.fori_loop(0, rows // batch, carry_step, jnp.zeros((batch, 2 * P), F32))
    y_ref[0] += _dot3_nt(xin_ref[...], qt_ref[0])


def _s5_scan(u, mt, pm, qt, al, d, batch):
    G, rows, width = u.shape
    P2 = 2 * S5_STATE
    per_g = lambda shp: pl.BlockSpec((1,) + shp, lambda g: (g, 0, 0))
    return pl.pallas_call(
        functools.partial(_s5_scan_kernel, batch=batch),
        grid=(G,),
        in_specs=[per_g((rows, width)), per_g((width, width)), per_g((width, P2)), per_g((width, P2)),
                  per_g((1, P2)), per_g((1, width))],
        out_specs=per_g((rows, width)),
        out_shape=jax.ShapeDtypeStruct((G, rows, width), F32),
        scratch_shapes=[pltpu.VMEM((rows, P2), F32)],
        compiler_params=_cparams("parallel"), name="s5_scan",
    )(u, mt, pm, qt, al, d)


def _even_out_kernel(x_ref, nsa_ref, y_ref, gw_ref, gb_ref, wa_ref, wb_ref, lg_ref, lb_ref, o_ref, *, alpha):
    g = _gelu_tanh(y_ref[...])
    s5 = g * _sigmoid(_dot(g.astype(BF16), gw_ref[...]) + gb_ref[...])
    mix = _dot(nsa_ref[...], wa_ref[...]) + _dot(s5.astype(BF16), wb_ref[...])
    o_ref[...] = _layer_norm(alpha * x_ref[...] + mix, lg_ref[...], lb_ref[...])


def _even_out(x2d, o_nsa, y_s5, glu_w, glu_b, wa, wb, ln_g, ln_b, alpha):
    T, D = x2d.shape
    tm = min(ROW_TILE, T)
    row = lambda w: pl.BlockSpec((tm, w), lambda i: (i, 0))
    whole = lambda a: pl.BlockSpec(a.shape, lambda i: (0, 0))
    return pl.pallas_call(
        functools.partial(_even_out_kernel, alpha=alpha),
        grid=(T // tm,),
        in_specs=[row(D), row(o_nsa.shape[1]), row(y_s5.shape[1]), whole(glu_w), whole(glu_b), whole(wa), whole(wb),
                  whole(ln_g), whole(ln_b)],
        out_specs=row(D), out_shape=jax.ShapeDtypeStruct((T, D), F32),
        compiler_params=_cparams("parallel"), name="even_out",
    )(x2d, o_nsa, y_s5, glu_w, glu_b, wa, wb, ln_g, ln_b)


def _odd_out_kernel(x_ref, a_ref, w_ref, lg_ref, lb_ref, o_ref, *, alpha):
    mix = _dot(a_ref[...], w_ref[...])
    o_ref[...] = _layer_norm(alpha * x_ref[...] + mix, lg_ref[...], lb_ref[...])


def _odd_out(x2d, attn, w, ln_g, ln_b, alpha):
    T, D = x2d.shape
    tm = min(ROW_TILE, T)
    row = lambda wd: pl.BlockSpec((tm, wd), lambda i: (i, 0))
    whole = lambda a: pl.BlockSpec(a.shape, lambda i: (0, 0))
    return pl.pallas_call(
        functools.partial(_odd_out_kernel, alpha=alpha),
        grid=(T // tm,),
        in_specs=[row(D), row(attn.shape[1]), whole(w), whole(ln_g), whole(ln_b)],
        out_specs=row(D), out_shape=jax.ShapeDtypeStruct((T, D), F32),
        compiler_params=_cparams("parallel"), name="odd_out",
    )(x2d, attn, w, ln_g, ln_b)


def _moba_kernel(q_ref, k_ref, v_ref, o_ref, kmean_ref, *, seq, top):
    BLK, dh = MOBA_BLOCK, HEAD_DIM
    nb = seq // BLK
    i = pl.program_id(2)

    @pl.when(i == 0)
    def _():
        kmean_ref[...] = jnp.mean(k_ref[0].astype(F32).reshape(nb, BLK, 2 * dh), axis=1)

    jf = lax.broadcasted_iota(I32, (BLK, nb), 1).astype(F32)
    i_f = i.astype(F32)
    r_idx = lax.broadcasted_iota(I32, (BLK, BLK), 0)
    c_idx = lax.broadcasted_iota(I32, (BLK, BLK), 1)
    causal = c_idx <= r_idx
    row0 = pl.multiple_of(i * BLK, BLK)

    qs, selms, init = [], [], []
    for h in range(2):
        hs = slice(h * dh, (h + 1) * dh)
        q = q_ref[0, :, hs] * jnp.asarray(dh ** -0.5, BF16)
        km_hi, km_lo = _split(kmean_ref[:, hs])
        gate = _dot_nt(q, km_hi) + _dot_nt(q, km_lo)
        gate = jnp.where(jf < i_f, gate, NEG)
        selm = jnp.zeros((BLK, nb), F32)
        for _ in range(top):
            m = jnp.max(gate, axis=-1, keepdims=True)
            idx = jnp.min(jnp.where(gate == m, jf, float(nb)), axis=-1, keepdims=True)
            hit = jf == idx
            selm = jnp.where(hit, jnp.where(m > 0.5 * NEG, 1.0, 0.0), selm)
            gate = jnp.where(hit, 2.0 * NEG, gate)
        s = _dot_nt(q, k_ref[0, pl.ds(row0, BLK), hs])
        sm = jnp.where(causal, s, NEG)
        m0 = jnp.max(sm, axis=-1, keepdims=True)
        p = jnp.where(causal, jnp.exp(sm - m0), 0.0)
        l0 = jnp.sum(p, axis=-1, keepdims=True)
        acc0 = _dot(p.astype(BF16), v_ref[0, pl.ds(row0, BLK), hs])
        qs.append(q)
        selms.append(selm)
        init += [m0, l0, acc0]

    def past_step(j, carry):
        out = []
        j_f = j.astype(F32)
        c0 = pl.multiple_of(j * BLK, BLK)
        for h in range(2):
            hs = slice(h * dh, (h + 1) * dh)
            m_i, l_i, acc = carry[3 * h:3 * h + 3]
            on = jnp.sum(jnp.where(jf == j_f, selms[h], 0.0), axis=-1, keepdims=True) > 0.5
            s = _dot_nt(qs[h], k_ref[0, pl.ds(c0, BLK), hs])
            m_new = jnp.maximum(m_i, jnp.max(jnp.where(on, s, NEG), axis=-1, keepdims=True))
            p = jnp.where(on, jnp.exp(s - m_new), 0.0)
            alpha = jnp.exp(m_i - m_new)
            l_new = alpha * l_i + jnp.sum(p, axis=-1, keepdims=True)
            acc_new = alpha * acc + _dot(p.astype(BF16), v_ref[0, pl.ds(c0, BLK), hs])
            out += [m_new, l_new, acc_new]
        return tuple(out)

    fin = lax.fori_loop(0, i, past_step, tuple(init))
    o_ref[0] = jnp.concatenate([fin[2] / fin[1], fin[5] / fin[4]], axis=-1).astype(o_ref.dtype)


def _moba_attention(q, k, v, seq):
    B, _, width = q.shape
    BLK = MOBA_BLOCK
    nb = seq // BLK
    top = min(MOBA_TOPK, max(nb - 1, 1))
    return pl.pallas_call(
        functools.partial(_moba_kernel, seq=seq, top=top),
        grid=(B, width // LANES, nb),
        in_specs=[pl.BlockSpec((1, BLK, LANES), lambda b, h, i: (b, i, h)),
                  pl.BlockSpec((1, seq, LANES), lambda b, h, i: (b, 0, h)),
                  pl.BlockSpec((1, seq, LANES), lambda b, h, i: (b, 0, h))],
        out_specs=pl.BlockSpec((1, BLK, LANES), lambda b, h, i: (b, i, h)),
        out_shape=jax.ShapeDtypeStruct((B, seq, width), BF16),
        scratch_shapes=[pltpu.VMEM((nb, LANES), F32)],
        compiler_params=_cparams("parallel", "parallel", "arbitrary"), name="moba_attn",
    )(q, k, v)


def _route(logits):
    n_fine = MOE_GROUPS * MOE_EXPERTS_PER_GROUP
    lane = lax.broadcasted_iota(I32, logits.shape, 1)
    lf = lane.astype(F32)
    is_coarse = (lane >= n_fine) & (lane < n_fine + MOE_GROUPS)
    pc = _masked_softmax(logits, is_coarse)
    gv = jnp.max(pc, axis=-1, keepdims=True)
    gidx = jnp.min(jnp.where(is_coarse & (pc == gv), lf - float(n_fine), float(MOE_GROUPS)), axis=-1, keepdims=True)
    in_group = (lane < n_fine) & ((lane // MOE_EXPERTS_PER_GROUP).astype(F32) == gidx)
    pf = _masked_softmax(logits, in_group)
    cand = jnp.where(in_group, pf, -1.0)
    m1 = jnp.max(cand, axis=-1, keepdims=True)
    i1 = jnp.min(jnp.where(cand == m1, lf, float(LANES)), axis=-1, keepdims=True)
    cand = jnp.where(lf == i1, -1.0, cand)
    m2 = jnp.max(cand, axis=-1, keepdims=True)
    i2 = jnp.min(jnp.where(cand == m2, lf, float(LANES)), axis=-1, keepdims=True)
    tot = m1 + m2
    return jnp.where(lf == i1, gv * (m1 / tot), jnp.where(lf == i2, gv * (m2 / tot), 0.0))


def _moe_kernel(x_ref, wr_ref, br_ref, wg_ref, wu_ref, wd_ref, lg_ref, lb_ref, o_ref, gates_ref, acc_ref, *, alpha):
    E, FF = MOE_EXPERTS_PER_GROUP, MOE_FF
    g = pl.program_id(1)
    x = x_ref[...]

    @pl.when(g == 0)
    def _():
        gates_ref[...] = _route(_dot3(x, wr_ref[...]) + br_ref[...])
        acc_ref[...] = jnp.zeros_like(acc_ref)

    xb = x.astype(BF16)
    hg = _dot(xb, wg_ref[0])
    hu = _dot(xb, wu_ref[0])
    h = hg * _sigmoid(hg) * hu
    gates = gates_ref[...]
    lane = lax.broadcasted_iota(I32, gates.shape, 1)
    parts = []
    for e in range(E):
        ge = jnp.sum(jnp.where(lane == g * E + e, gates, 0.0), axis=-1, keepdims=True)
        parts.append((h[:, e * FF:(e + 1) * FF] * ge).astype(BF16))
    acc_ref[...] += _dot(jnp.concatenate(parts, axis=-1), wd_ref[0])

    @pl.when(g == pl.num_programs(1) - 1)
    def _():
        o_ref[...] = _layer_norm(alpha * x + acc_ref[...], lg_ref[...], lb_ref[...])


def _moe(x2d, wr, br, wg, wu, wd, ln_g, ln_b, alpha):
    T, D = x2d.shape
    tm = min(ROW_TILE, T)
    NG = wg.shape[0]
    whole = lambda a: pl.BlockSpec(a.shape, lambda i, g: (0, 0))
    per_g = lambda a: pl.BlockSpec((1,) + a.shape[1:], lambda i, g: (g, 0, 0))
    return pl.pallas_call(
        functools.partial(_moe_kernel, alpha=alpha),
        grid=(T // tm, NG),
        in_specs=[pl.BlockSpec((tm, D), lambda i, g: (i, 0)), whole(wr), whole(br), per_g(wg), per_g(wu), per_g(wd),
                  whole(ln_g), whole(ln_b)],
        out_specs=pl.BlockSpec((tm, D), lambda i, g: (i, 0)),
        out_shape=jax.ShapeDtypeStruct((T, D), F32),
        scratch_shapes=[pltpu.VMEM((tm, LANES), F32), pltpu.VMEM((tm, D), F32)],
        compiler_params=_cparams("parallel", "arbitrary"), name="moe",
    )(x2d, wr, br, wg, wu, wd, ln_g, ln_b)


def _rope_tables(seq):
    inv = 1.0 / (ROPE_THETA ** (jnp.arange(0, HEAD_DIM, 2, dtype=F32) / HEAD_DIM))
    ang = jnp.arange(seq, dtype=F32)[:, None] * inv[None, :]
    cos, sin = jnp.cos(ang), jnp.sin(ang)
    reps = LANES // HEAD_DIM
    return jnp.tile(jnp.concatenate([cos, cos], -1), (1, reps)), jnp.tile(jnp.concatenate([-sin, sin], -1), (1, reps))


def _selection_constants(seq):
    ncp = seq // CMP_STRIDE
    nbs = seq // SEL_BLOCK
    starts = jnp.arange(ncp) * CMP_STRIDE
    bstart = jnp.arange(nbs) * SEL_BLOCK
    ovl = jnp.clip(jnp.minimum(starts[:, None] + CMP_BLOCK, bstart[None, :] + SEL_BLOCK)
                   - jnp.maximum(starts[:, None], bstart[None, :]), 0, CMP_BLOCK)
    c2s = (ovl.astype(F32) / CMP_BLOCK).astype(BF16)
    expand = (jnp.arange(seq)[None, :] // SEL_BLOCK == jnp.arange(nbs)[:, None]).astype(BF16)
    return c2s, expand


def _moe_weights(w_coarse, b_coarse, w_fine, b_fine, w_gate, w_up, w_down):
    D = w_coarse.shape[0]
    NG, E, _, FF = w_gate.shape
    n_fine = NG * E
    wr = jnp.zeros((D, LANES), F32)
    wr = wr.at[:, :n_fine].set(w_fine.transpose(1, 0, 2).reshape(D, n_fine)).at[:, n_fine:n_fine + NG].set(w_coarse)
    br = jnp.zeros((1, LANES), F32)
    br = br.at[0, :n_fine].set(b_fine.reshape(n_fine)).at[0, n_fine:n_fine + NG].set(b_coarse)
    wg = w_gate.transpose(0, 2, 1, 3).reshape(NG, D, E * FF).astype(BF16)
    wu = w_up.transpose(0, 2, 1, 3).reshape(NG, D, E * FF).astype(BF16)
    wd = w_down.reshape(NG, E * FF, D).astype(BF16)
    return wr, br, wg, wu, wd


def _even_mixer(x2d, B, S, cos2, sin2, w_in, pe_k, pe_v, k_w1, k_w2, v_w1, v_w2,
                lam_re, lam_im, log_step, b_re, b_im, c_re, c_im, d_skip, glu_w, glu_b, w_out, ln_g, ln_b, alpha):
    G, Z, dh = NSA_KV_GROUPS, NSA_HEADS_PER_GROUP, HEAD_DIM
    kvw = NSA_KV_WIDTH
    D = x2d.shape[1]
    cuts = [NSA_WIDTH + i * kvw for i in range(7)]
    w_q, w_kc, w_vc, w_ks, w_vs, w_kw, w_vw = [w_in[:, a:b] for a, b in zip([0] + cuts[:-1], cuts)]
    n_gate = 3 * NSA_HEADS
    w_g = jnp.pad(w_in[:, cuts[-1]:cuts[-1] + n_gate], ((0, 0), (0, LANES - n_gate)))
    w_u = w_in[:, cuts[-1] + n_gate:]
    weights = [jnp.concatenate([w_q, w_kc, w_ks, w_kw], 1).astype(BF16),
               jnp.concatenate([w_vc, w_vs, w_vw], 1).astype(BF16), w_u.astype(BF16), w_g.astype(BF16)]
    qk, vv, u, gates = _project(x2d, cos2, sin2, weights, ["rope", "plain", "plain", "sigmoid"],
                                [BF16, BF16, F32, F32], S)

    def heads(t):
        return t.reshape(B, S, G, dh).transpose(0, 2, 1, 3)

    q = qk[:, :NSA_WIDTH].reshape(B, S, G, Z, dh).transpose(0, 2, 3, 1, 4)
    kc, ks, kw = [heads(qk[:, NSA_WIDTH + i * kvw:NSA_WIDTH + (i + 1) * kvw]) for i in range(3)]
    vc, vs, vw = [heads(vv[:, i * kvw:(i + 1) * kvw]) for i in range(3)]
    gates = gates[:, :n_gate].reshape(B, S, G, 3 * Z).transpose(0, 2, 1, 3)

    n_rows = S // CMP_STRIDE
    r = jnp.stack([kc, vc]).reshape(2, B * G, n_rows, CMP_STRIDE * dh)
    pe = jnp.stack([pe_k, pe_v]).reshape(2, 1, CMP_BLOCK * dh)
    w1 = jnp.stack([k_w1, v_w1]).astype(BF16)
    w2 = jnp.stack([k_w2, v_w2]).astype(BF16)
    cmp, cmp_t = _nsa_compress(r, pe, w1, w2, w2.transpose(0, 2, 1))
    kct = cmp_t[0].reshape(B, G, dh, n_rows)
    vcm = cmp[1].reshape(B, G, n_rows, dh)

    c2s, expand = _selection_constants(S)
    o_nsa = _nsa_attention(q, kct, vcm, ks.transpose(0, 1, 3, 2), vs, kw.transpose(0, 1, 3, 2), vw, gates,
                           c2s, expand, S)

    C, L = S5_GROUP_CH, S5_CHUNK
    SG = u.shape[1] // C
    mt, pm, qt, al = _s5_operators(lam_re[:, None, :], lam_im[:, None, :], log_step[:, None, None],
                                   b_re.transpose(0, 2, 1), b_im.transpose(0, 2, 1), c_re, c_im)
    u_g = u.reshape(B, S // L, L, SG, C).transpose(3, 1, 0, 2, 4).reshape(SG, (S // L) * B, L * C)
    d_t = jnp.tile(d_skip.reshape(SG, 1, C), (1, 1, L))
    y = _s5_scan(u_g, mt, pm, qt, al, d_t, B)
    y = y.reshape(SG, S // L, B, L, C).transpose(2, 1, 3, 0, 4).reshape(B * S, SG * C)

    return _even_out(x2d, o_nsa.reshape(B * S, NSA_WIDTH), y, glu_w.astype(BF16), glu_b[None, :],
                     w_out[:NSA_WIDTH].astype(BF16), w_out[NSA_WIDTH:].astype(BF16), ln_g[None, :], ln_b[None, :], alpha)


def _odd_mixer(x2d, B, S, cos2, sin2, w_in, w_out, ln_g, ln_b, alpha):
    D = x2d.shape[1]
    qk, v = _project(x2d, cos2, sin2, [w_in[:, :2 * D].astype(BF16), w_in[:, 2 * D:].astype(BF16)],
                     ["rope", "plain"], [BF16, BF16], S)
    q = qk[:, :D].reshape(B, S, D)
    k = qk[:, D:].reshape(B, S, D)
    attn = _moba_attention(q, k, v.reshape(B, S, D), S)
    return _odd_out(x2d, attn.reshape(B * S, D), w_out.astype(BF16), ln_g[None, :], ln_b[None, :], alpha)


def kernel(x, ev_w_in, nsa_pe_k, nsa_pe_v, nsa_cmp_k_w1, nsa_cmp_k_w2, nsa_cmp_v_w1, nsa_cmp_v_w2, s5_lambda_re, s5_lambda_im, s5_log_step, s5_b_re, s5_b_im, s5_c_re, s5_c_im, s5_d, s5_glu_w, s5_glu_b, ev_w_out, od_w_in, od_w_out, ln_mix_g, ln_mix_b, ln_ffn_g, ln_ffn_b, moe_w_coarse, moe_b_coarse, moe_w_fine, moe_b_fine, moe_w_gate, moe_w_up, moe_w_down):
    B, S, D = x.shape
    depth = ln_mix_g.shape[0]
    alpha = (2.0 * depth) ** 0.25
    cos2, sin2 = _rope_tables(S)
    h = x.reshape(B * S, D)
    for layer in range(depth):
        if layer % 2 == 0:
            e = layer // 2
            h = _even_mixer(h, B, S, cos2, sin2, ev_w_in[e], nsa_pe_k[e], nsa_pe_v[e], nsa_cmp_k_w1[e], nsa_cmp_k_w2[e],
                            nsa_cmp_v_w1[e], nsa_cmp_v_w2[e], s5_lambda_re[e], s5_lambda_im[e], s5_log_step[e],
                            s5_b_re[e], s5_b_im[e], s5_c_re[e], s5_c_im[e], s5_d[e], s5_glu_w[e], s5_glu_b[e],
                            ev_w_out[e], ln_mix_g[layer], ln_mix_b[layer], alpha)
        else:
            o = layer // 2
            h = _odd_mixer(h, B, S, cos2, sin2, od_w_in[o], od_w_out[o], ln_mix_g[layer], ln_mix_b[layer], alpha)
        wr, br, wg, wu, wd = _moe_weights(moe_w_coarse[layer], moe_b_coarse[layer], moe_w_fine[layer], moe_b_fine[layer],
                                          moe_w_gate[layer], moe_w_up[layer], moe_w_down[layer])
        h = _moe(h, wr, br, wg, wu, wd, ln_ffn_g[layer][None, :], ln_ffn_b[layer][None, :], alpha)
    return h.reshape(B, S, D)
```

```python
import functools
import math

import jax
import jax.numpy as jnp
from jax import lax
from jax.experimental import pallas as pl
from jax.experimental.pallas import tpu as pltpu

F32 = jnp.float32
BF16 = jnp.bfloat16
I32 = jnp.int32

HEAD_DIM = 64
ROPE_THETA = 10000.0
LN_EPS = 1e-5

NSA_KV_GROUPS = 2
NSA_HEADS_PER_GROUP = 4
NSA_HEADS = NSA_KV_GROUPS * NSA_HEADS_PER_GROUP
NSA_WIDTH = NSA_HEADS * HEAD_DIM
NSA_KV_WIDTH = NSA_KV_GROUPS * HEAD_DIM
CMP_BLOCK = 32
CMP_STRIDE = 16
SEL_BLOCK = 64
SEL_TOPN = 8
WINDOW = 256
NSA_QCHUNK = 128
NSA_KV_TILE = 512

S5_GROUP_CH = 16
S5_STATE = 64
S5_CHUNK = 64

MOBA_BLOCK = 256
MOBA_TOPK = 3

MOE_GROUPS = 4
MOE_EXPERTS_PER_GROUP = 8
MOE_FF = 128

LANES = 128
SUBLANES = 8
ROW_TILE = 512
VMEM_LIMIT = 56 * 1024 * 1024
NEG = -1e30


def _cparams(*sem):
    return pltpu.CompilerParams(dimension_semantics=sem, vmem_limit_bytes=VMEM_LIMIT)


def _dot(a, b):
    return jnp.dot(a, b, preferred_element_type=F32)


def _dot_nt(a, b):
    return lax.dot_general(a, b, (((1,), (1,)), ((), ())), preferred_element_type=F32)


def _split(a):
    hi = a.astype(BF16)
    lo = (a - hi.astype(F32)).astype(BF16)
    return hi, lo


def _dot3(a, b):
    ah, al = _split(a)
    bh, bl = _split(b)
    return _dot(ah, bh) + _dot(ah, bl) + _dot(al, bh)


def _dot3_nt(a, b):
    ah, al = _split(a)
    bh, bl = _split(b)
    return _dot_nt(ah, bh) + _dot_nt(ah, bl) + _dot_nt(al, bh)


def _sigmoid(x):
    return 1.0 / (1.0 + jnp.exp(-x))


def _gelu_tanh(x):
    return 0.5 * x * (1.0 + jnp.tanh(math.sqrt(2.0 / math.pi) * (x + 0.044715 * (x * x * x))))


def _layer_norm(r, g, b):
    mu = jnp.mean(r, axis=-1, keepdims=True)
    d = r - mu
    var = jnp.mean(d * d, axis=-1, keepdims=True)
    return d * lax.rsqrt(var + LN_EPS) * g + b


def _masked_softmax(s, ok):
    sm = jnp.where(ok, s, NEG)
    m = jnp.max(sm, axis=-1, keepdims=True)
    p = jnp.where(ok, jnp.exp(sm - m), 0.0)
    return p / jnp.maximum(jnp.sum(p, axis=-1, keepdims=True), 1e-30)


def _rope_128(a, cos2, sin2):
    lane = lax.broadcasted_iota(I32, a.shape, 1)
    first_half = (lane & (HEAD_DIM - 1)) < (HEAD_DIM // 2)
    partner = jnp.where(first_half, pltpu.roll(a, LANES - HEAD_DIM // 2, 1), pltpu.roll(a, HEAD_DIM // 2, 1))
    return a * cos2 + partner * sin2


def _proj_kernel(x_ref, cos_ref, sin_ref, *refs, kinds):
    n = len(kinds)
    w_refs, o_refs = refs[:n], refs[n:]
    xb = x_ref[...].astype(BF16)
    cos2 = cos_ref[...]
    sin2 = sin_ref[...]
    for kind, w_ref, o_ref in zip(kinds, w_refs, o_refs):
        width = w_ref.shape[1]
        for c0 in range(0, width, LANES):
            acc = _dot(xb, w_ref[:, c0:c0 + LANES])
            if kind == "rope":
                acc = _rope_128(acc, cos2, sin2)
            elif kind == "sigmoid":
                acc = _sigmoid(acc)
            o_ref[:, c0:c0 + LANES] = acc.astype(o_ref.dtype)


def _project(x2d, cos2, sin2, weights, kinds, dtypes, seq):
    T, D = x2d.shape
    tm = min(ROW_TILE, seq)
    pos_tiles = seq // tm
    in_specs = [pl.BlockSpec((tm, D), lambda i: (i, 0)),
                pl.BlockSpec((tm, LANES), lambda i: (i % pos_tiles, 0)),
                pl.BlockSpec((tm, LANES), lambda i: (i % pos_tiles, 0))]
    in_specs += [pl.BlockSpec(w.shape, lambda i: (0, 0)) for w in weights]
    out_specs = [pl.BlockSpec((tm, w.shape[1]), lambda i: (i, 0)) for w in weights]
    out_shape = [jax.ShapeDtypeStruct((T, w.shape[1]), dt) for w, dt in zip(weights, dtypes)]
    return pl.pallas_call(
        functools.partial(_proj_kernel, kinds=tuple(kinds)),
        grid=(T // tm,), in_specs=in_specs, out_specs=out_specs, out_shape=out_shape,
        compiler_params=_cparams("parallel"), name="proj",
    )(x2d, cos2, sin2, *weights)


def _compress_kernel(r_ref, pe_ref, w1_ref, w2_ref, w2t_ref, o_ref, ot_ref):
    half = CMP_STRIDE * HEAD_DIM
    r = r_ref[0, 0]
    n = r.shape[0]
    top = _dot(r, w1_ref[0, :half, :])
    bot = _dot(r, w1_ref[0, half:, :])
    bias = _dot(pe_ref[0].astype(BF16), w1_ref[0])
    h = top + pltpu.roll(bot, n - 1, 0) + bias
    hb = _gelu_tanh(h).astype(BF16)
    o_ref[0, 0] = _dot(hb, w2_ref[0]).astype(o_ref.dtype)
    ot_ref[0, 0] = _dot_nt(w2t_ref[0], hb).astype(ot_ref.dtype)


def _nsa_compress(r, pe, w1, w2, w2t):
    _, bg, n, width = r.shape
    hid = w1.shape[2]
    return pl.pallas_call(
        _compress_kernel,
        grid=(2, bg),
        in_specs=[pl.BlockSpec((1, 1, n, width), lambda a, i: (a, i, 0, 0)),
                  pl.BlockSpec((1, 1, 2 * width), lambda a, i: (a, 0, 0)),
                  pl.BlockSpec((1, 2 * width, hid), lambda a, i: (a, 0, 0)),
                  pl.BlockSpec((1, hid, HEAD_DIM), lambda a, i: (a, 0, 0)),
                  pl.BlockSpec((1, HEAD_DIM, hid), lambda a, i: (a, 0, 0))],
        out_specs=[pl.BlockSpec((1, 1, n, HEAD_DIM), lambda a, i: (a, i, 0, 0)),
                   pl.BlockSpec((1, 1, HEAD_DIM, n), lambda a, i: (a, i, 0, 0))],
        out_shape=[jax.ShapeDtypeStruct((2, bg, n, HEAD_DIM), BF16),
                   jax.ShapeDtypeStruct((2, bg, HEAD_DIM, n), BF16)],
        compiler_params=_cparams("parallel", "parallel"), name="nsa_compress",
    )(r, pe, w1, w2, w2t)


def _softmax_cols(s, ok):
    sm = jnp.where(ok, s, NEG)
    m = jnp.max(sm, axis=0, keepdims=True)
    p = jnp.where(ok, jnp.exp(sm - m), 0.0)
    return p / jnp.maximum(jnp.sum(p, axis=0, keepdims=True), 1e-30)


def _nsa_attn_kernel(qt_ref, kc_ref, vct_ref, ks_ref, vst_ref, kw_ref, vwt_ref, gt_ref, c2st_ref, o_ref,
                     sel_ref, s_ref, p_ref, *, seq, n_sel):
    Z, dh, QC, SB = NSA_HEADS_PER_GROUP, HEAD_DIM, NSA_QCHUNK, SEL_BLOCK
    W = Z * QC
    KT = 2 * SB
    ncp = seq // CMP_STRIDE
    nbs = seq // SEL_BLOCK
    c = pl.program_id(2)
    t0 = c * QC
    qt = qt_ref[0, 0, 0] * jnp.asarray(dh ** -0.5, BF16)
    qpos = t0 + (lax.broadcasted_iota(I32, (1, W), 1) & (QC - 1))
    d0 = pl.multiple_of(t0, QC)
    band = QC + WINDOW
    w0 = pl.multiple_of(jnp.maximum(t0 - WINDOW, 0), QC)

    def tile4(row):
        return jnp.concatenate([row] * Z, axis=1)

    s_c = _dot(kc_ref[0, 0], qt)
    s_w = _dot(kw_ref[0, 0, pl.ds(w0, band), :], qt)
    s_d = _dot(ks_ref[0, 0, pl.ds(d0, KT), :], qt)
    s_ref[0] = _dot(ks_ref[0, 0, pl.ds(0, KT), :], qt)

    cmp_end = lax.broadcasted_iota(I32, (ncp, 1), 0) * CMP_STRIDE + (CMP_BLOCK - 1)
    p_c = _softmax_cols(s_c, cmp_end <= qpos)
    o_c = _dot(vct_ref[0, 0], p_c.astype(BF16))

    p_sum = p_c[:, 0:QC]
    for z in range(1, Z):
        p_sum = p_sum + p_c[:, z * QC:(z + 1) * QC]
    p_hi, p_lo = _split(p_sum)
    imp = _dot(c2st_ref[...], p_hi) + _dot(c2st_ref[...], p_lo)

    jf = lax.broadcasted_iota(I32, (nbs, QC), 0).astype(F32)
    qblk = (qpos[:, :QC] // SEL_BLOCK).astype(F32)
    future = jf > qblk
    forced = (jf == 0.0) | (jf == qblk) | (jf == qblk - 1.0)
    score = jnp.where(future, -1.0, jnp.where(forced, 1e3, imp))
    selm = jnp.zeros((nbs, QC), F32)
    for _ in range(n_sel):
        m = jnp.max(score, axis=0, keepdims=True)
        idx = jnp.min(jnp.where(score == m, jf, float(nbs)), axis=0, keepdims=True)
        hit = jf == idx
        selm = jnp.where(hit, jnp.where(m >= 0.0, 1.0, 0.0), selm)
        score = jnp.where(hit, NEG, score)
    selw = tile4(selm)
    sel_ref[...] = jnp.where(lax.broadcasted_iota(I32, (nbs, W), 0) < 2 * c, selw, 0.0)

    brow = lax.broadcasted_iota(I32, (nbs, 1), 0)
    on_a = jnp.sum(jnp.where(brow == 2 * c, selw, 0.0), axis=0, keepdims=True)
    on_b = jnp.sum(jnp.where(brow == 2 * c + 1, selw, 0.0), axis=0, keepdims=True)
    krow = lax.broadcasted_iota(I32, (KT, 1), 0)
    ok_d = jnp.where(t0 + krow <= qpos, jnp.where(krow < SB, on_a, on_b), 0.0) > 0.5
    sm = jnp.where(ok_d, s_d, NEG)
    m0 = jnp.max(sm, axis=0, keepdims=True)
    p_d = jnp.where(ok_d, jnp.exp(sm - m0), 0.0)
    l0 = jnp.sum(p_d, axis=0, keepdims=True)
    acc0 = _dot(vst_ref[0, 0, :, pl.ds(d0, KT)], p_d.astype(BF16))
    p_ref[0] = jnp.zeros((KT, W), BF16)

    dist = qpos - (w0 + lax.broadcasted_iota(I32, (band, 1), 0))
    p_w = _softmax_cols(s_w, jnp.where(dist >= 0, dist, WINDOW) < WINDOW)
    o_w = _dot(vwt_ref[0, 0, :, pl.ds(w0, band)], p_w.astype(BF16))

    def stage(t, src, dst, carry):
        m_i, l_i, acc = carry
        tp = jnp.maximum(t - 1, 0)
        k_next = pl.multiple_of(jnp.minimum(t + 1, c) * KT, KT)
        s_ref[dst] = _dot(ks_ref[0, 0, pl.ds(k_next, KT), :], qt)
        vt = vst_ref[0, 0, :, pl.ds(pl.multiple_of(tp * KT, KT), KT)]
        pv_a = _dot(vt[:, :SB], p_ref[src, 0:SB, :])
        pv_b = _dot(vt[:, SB:], p_ref[src, SB:KT, :])
        on_a = sel_ref[pl.ds(2 * t, 1), :] > 0.5
        on_b = sel_ref[pl.ds(2 * t + 1, 1), :] > 0.5
        prev_a = sel_ref[pl.ds(2 * tp, 1), :] > 0.5
        prev_b = sel_ref[pl.ds(2 * tp + 1, 1), :] > 0.5
        mx_a = jnp.max(s_ref[src, 0:SB, :], axis=0, keepdims=True)
        mx_b = jnp.max(s_ref[src, SB:KT, :], axis=0, keepdims=True)
        m_new = jnp.maximum(m_i, jnp.maximum(jnp.where(on_a, mx_a, NEG), jnp.where(on_b, mx_b, NEG)))
        p = jnp.exp(s_ref[src] - m_new)
        p_ref[dst] = p.astype(BF16)
        alpha = jnp.exp(m_i - m_new)
        l_new = (alpha * l_i + jnp.where(on_a, jnp.sum(p[:SB], axis=0, keepdims=True), 0.0)
                 + jnp.where(on_b, jnp.sum(p[SB:], axis=0, keepdims=True), 0.0))
        acc_new = alpha * (acc + jnp.where(prev_a, pv_a, 0.0) + jnp.where(prev_b, pv_b, 0.0))
        return m_new, l_new, acc_new

    n_pairs = (c + 1) // 2
    _, l_s, acc_s = lax.fori_loop(0, n_pairs, lambda u, cr: stage(2 * u + 1, 1, 0, stage(2 * u, 0, 1, cr)),
                                  (m0, l0, acc0))
    tl = jnp.maximum(2 * n_pairs - 1, 0)
    vt = vst_ref[0, 0, :, pl.ds(pl.multiple_of(tl * KT, KT), KT)]
    last_a = sel_ref[pl.ds(2 * tl, 1), :] > 0.5
    last_b = sel_ref[pl.ds(2 * tl + 1, 1), :] > 0.5
    acc_s = (acc_s + jnp.where(last_a, _dot(vt[:, :SB], p_ref[0, 0:SB, :]), 0.0)
             + jnp.where(last_b, _dot(vt[:, SB:], p_ref[0, SB:KT, :]), 0.0))
    o_s = acc_s / jnp.maximum(l_s, 1e-30)

    gt = gt_ref[0, 0]
    g_c = jnp.concatenate([gt[3 * z:3 * z + 1] for z in range(Z)], axis=1)
    g_s = jnp.concatenate([gt[3 * z + 1:3 * z + 2] for z in range(Z)], axis=1)
    g_w = jnp.concatenate([gt[3 * z + 2:3 * z + 3] for z in range(Z)], axis=1)
    o_ref[0, 0, 0] = (g_c * o_c + g_s * o_s + g_w * o_w).astype(o_ref.dtype)


def _nsa_attention(qt, kc, vct, ks, vst, kw, vwt, gates_t, c2st, seq):
    B, G, n_chunks, dh, W = qt.shape
    QC = NSA_QCHUNK
    KT = 2 * SEL_BLOCK
    ncp = seq // CMP_STRIDE
    nbs = seq // SEL_BLOCK
    full = lambda shp: pl.BlockSpec((1, 1) + shp, lambda b, g, c: (b, g, 0, 0))
    return pl.pallas_call(
        functools.partial(_nsa_attn_kernel, seq=seq, n_sel=min(SEL_TOPN, nbs)),
        grid=(B, G, n_chunks),
        in_specs=[pl.BlockSpec((1, 1, 1, dh, W), lambda b, g, c: (b, g, c, 0, 0)),
                  full((ncp, dh)), full((dh, ncp)), full((seq, dh)), full((dh, seq)), full((seq, dh)), full((dh, seq)),
                  pl.BlockSpec((1, 1, gates_t.shape[2], QC), lambda b, g, c: (b, g, 0, c)),
                  pl.BlockSpec((nbs, ncp), lambda b, g, c: (0, 0))],
        out_specs=pl.BlockSpec((1, 1, 1, dh, W), lambda b, g, c: (b, g, c, 0, 0)),
        out_shape=jax.ShapeDtypeStruct((B, G, n_chunks, dh, W), BF16),
        scratch_shapes=[pltpu.VMEM((nbs, W), F32), pltpu.VMEM((2, KT, W), F32), pltpu.VMEM((2, KT, W), BF16)],
        compiler_params=_cparams("parallel", "parallel", "arbitrary"), name="nsa_attn",
    )(qt, kc, vct, ks, vst, kw, vwt, gates_t, c2st)


def _cmul(ar, ai, br, bi):
    return ar * br - ai * bi, ar * bi + ai * br


def _s5_ops_kernel(lre_ref, lim_ref, lstep_ref, bret_ref, bimt_ref, cre_ref, cim_ref,
                   mt_ref, pm_ref, qt_ref, al_ref):
    L, C, P = S5_CHUNK, S5_GROUP_CH, S5_STATE
    lr = lre_ref[0]
    li = lim_ref[0]
    step = jnp.exp(lstep_ref[0])
    mag = jnp.exp(lr * step)
    ar = mag * jnp.cos(li * step)
    ai = mag * jnp.sin(li * step)
    den = lr * lr + li * li
    fr = ((ar - 1.0) * lr + ai * li) / den
    fi = (ai * lr - (ar - 1.0) * li) / den
    bbr, bbi = _cmul(fr, fi, bret_ref[0], bimt_ref[0])

    up_r, up_i = jnp.ones((1, 1, P), F32), jnp.zeros((1, 1, P), F32)
    dn_r, dn_i = up_r, up_i
    sr, si = ar.reshape(1, 1, P), ai.reshape(1, 1, P)
    a1r, a1i = sr, si
    n = 1
    while n < L:
        tr, ti = _cmul(up_r, up_i, sr, si)
        up_r, up_i = jnp.concatenate([up_r, tr], 0), jnp.concatenate([up_i, ti], 0)
        tr, ti = _cmul(dn_r, dn_i, sr, si)
        dn_r, dn_i = jnp.concatenate([tr, dn_r], 0), jnp.concatenate([ti, dn_i], 0)
        sr, si = _cmul(sr, si, sr, si)
        n *= 2
    al_ref[0] = jnp.concatenate([sr.reshape(1, P), si.reshape(1, P)], -1)

    cr = cre_ref[0][None]
    ci = cim_ref[0][None]
    nr, ni = _cmul(up_r, up_i, a1r, a1i)
    wr, wi = _cmul(cr, ci, nr, ni)
    qt_ref[0] = jnp.concatenate([wr.reshape(L * C, P), -wi.reshape(L * C, P)], -1)
    er, ei = _cmul(dn_r, dn_i, bbr[None], bbi[None])
    pm_ref[0] = jnp.concatenate([er.reshape(L * C, P), ei.reshape(L * C, P)], -1).astype(pm_ref.dtype)
    w0r, w0i = _cmul(cr, ci, up_r, up_i)
    kt = _dot3_nt(bbr, w0r.reshape(L * C, P)) - _dot3_nt(bbi, w0i.reshape(L * C, P))
    lane = lax.broadcasted_iota(I32, kt.shape, 1)
    mt_ref[0, 0:C, :] = kt.astype(mt_ref.dtype)
    for s in range(1, L):
        shifted = jnp.where(lane >= s * C, pltpu.roll(kt, s * C, 1), 0.0)
        mt_ref[0, s * C:(s + 1) * C, :] = shifted.astype(mt_ref.dtype)


def _s5_operators(lre, lim, lstep, bret, bimt, cre, cim):
    G = lre.shape[0]
    L, C, P = S5_CHUNK, S5_GROUP_CH, S5_STATE
    vec = pl.BlockSpec((1, 1, P), lambda g: (g, 0, 0))
    mat = pl.BlockSpec((1, C, P), lambda g: (g, 0, 0))
    return pl.pallas_call(
        _s5_ops_kernel,
        grid=(G,),
        in_specs=[vec, vec, pl.BlockSpec((1, 1, 1), lambda g: (g, 0, 0)), mat, mat, mat, mat],
        out_specs=[pl.BlockSpec((1, L * C, L * C), lambda g: (g, 0, 0)),
                   pl.BlockSpec((1, L * C, 2 * P), lambda g: (g, 0, 0)),
                   pl.BlockSpec((1, L * C, 2 * P), lambda g: (g, 0, 0)),
                   pl.BlockSpec((1, 1, 2 * P), lambda g: (g, 0, 0))],
        out_shape=[jax.ShapeDtypeStruct((G, L * C, L * C), BF16),
                   jax.ShapeDtypeStruct((G, L * C, 2 * P), BF16),
                   jax.ShapeDtypeStruct((G, L * C, 2 * P), F32),
                   jax.ShapeDtypeStruct((G, 1, 2 * P), F32)],
        compiler_params=_cparams("parallel"), name="s5_operators",
    )(lre, lim, lstep, bret, bimt, cre, cim)


def _s5_scan_kernel(u_ref, mt_ref, pm_ref, qt_ref, al_ref, d_ref, y_ref, xin_ref, *, batch):
    P = S5_STATE
    rows = u_ref.shape[1]
    u = u_ref[0]
    ub = u.astype(BF16)
    y_ref[0] = _dot(ub, mt_ref[0]) + d_ref[0] * u
    xloc = _dot(ub, pm_ref[0])
    al = al_ref[0]
    lane = lax.broadcasted_iota(I32, (1, 2 * P), 1)
    mul_same = jnp.where(lane < P, al, pltpu.roll(al, P, 1))
    mul_swap = jnp.where(lane < P, -pltpu.roll(al, P, 1), al)
    xin_ref[...] = xloc

    def carry_step(k, state):
        r0 = pl.multiple_of(k * batch, batch)
        loc = xin_ref[pl.ds(r0, batch), :]
        xin_ref[pl.ds(r0, batch), :] = state
        return state * mul_same + pltpu.roll(state, P, 1) * mul_swap + loc

    lax.fori_loop(0, rows // batch, carry_step, jnp.zeros((batch, 2 * P), F32))
    y_ref[0] += _dot3_nt(xin_ref[...], qt_ref[0])


def _s5_scan(u, mt, pm, qt, al, d, batch):
    G, rows, width = u.shape
    P2 = 2 * S5_STATE
    per_g = lambda shp: pl.BlockSpec((1,) + shp, lambda g: (g, 0, 0))
    return pl.pallas_call(
        functools.partial(_s5_scan_kernel, batch=batch),
        grid=(G,),
        in_specs=[per_g((rows, width)), per_g((width, width)), per_g((width, P2)), per_g((width, P2)),
                  per_g((1, P2)), per_g((1, width))],
        out_specs=per_g((rows, width)),
        out_shape=jax.ShapeDtypeStruct((G, rows, width), F32),
        scratch_shapes=[pltpu.VMEM((rows, P2), F32)],
        compiler_params=_cparams("parallel"), name="s5_scan",
    )(u, mt, pm, qt, al, d)


def _even_out_kernel(x_ref, nsa_ref, y_ref, gw_ref, gb_ref, wa_ref, wb_ref, lg_ref, lb_ref, o_ref, *, alpha):
    g = _gelu_tanh(y_ref[...])
    s5 = g * _sigmoid(_dot(g.astype(BF16), gw_ref[...]) + gb_ref[...])
    mix = _dot(nsa_ref[...], wa_ref[...]) + _dot(s5.astype(BF16), wb_ref[...])
    o_ref[...] = _layer_norm(alpha * x_ref[...] + mix, lg_ref[...], lb_ref[...])


def _even_out(x2d, o_nsa, y_s5, glu_w, glu_b, wa, wb, ln_g, ln_b, alpha):
    T, D = x2d.shape
    tm = min(ROW_TILE, T)
    row = lambda w: pl.BlockSpec((tm, w), lambda i: (i, 0))
    whole = lambda a: pl.BlockSpec(a.shape, lambda i: (0, 0))
    return pl.pallas_call(
        functools.partial(_even_out_kernel, alpha=alpha),
        grid=(T // tm,),
        in_specs=[row(D), row(o_nsa.shape[1]), row(y_s5.shape[1]), whole(glu_w), whole(glu_b), whole(wa), whole(wb),
                  whole(ln_g), whole(ln_b)],
        out_specs=row(D), out_shape=jax.ShapeDtypeStruct((T, D), F32),
        compiler_params=_cparams("parallel"), name="even_out",
    )(x2d, o_nsa, y_s5, glu_w, glu_b, wa, wb, ln_g, ln_b)


def _odd_out_kernel(x_ref, a_ref, w_ref, lg_ref, lb_ref, o_ref, *, alpha):
    mix = _dot(a_ref[...], w_ref[...])
    o_ref[...] = _layer_norm(alpha * x_ref[...] + mix, lg_ref[...], lb_ref[...])


def _odd_out(x2d, attn, w, ln_g, ln_b, alpha):
    T, D = x2d.shape
    tm = min(ROW_TILE, T)
    row = lambda wd: pl.BlockSpec((tm, wd), lambda i: (i, 0))
    whole = lambda a: pl.BlockSpec(a.shape, lambda i: (0, 0))
    return pl.pallas_call(
        functools.partial(_odd_out_kernel, alpha=alpha),
        grid=(T // tm,),
        in_specs=[row(D), row(attn.shape[1]), whole(w), whole(ln_g), whole(ln_b)],
        out_specs=row(D), out_shape=jax.ShapeDtypeStruct((T, D), F32),
        compiler_params=_cparams("parallel"), name="odd_out",
    )(x2d, attn, w, ln_g, ln_b)


def _moba_kernel(qt_ref, k_ref, vt_ref, o_ref, kmean_ref, sel_ref, s_ref, p_ref, *, seq, top):
    BLK, dh = MOBA_BLOCK, HEAD_DIM
    nb = seq // BLK
    i = pl.program_id(2)

    @pl.when(i == 0)
    def _():
        kmean_ref[...] = jnp.mean(k_ref[0].astype(F32).reshape(nb, BLK, 2 * dh), axis=1)

    jf = lax.broadcasted_iota(I32, (nb, BLK), 0).astype(F32)
    i_f = i.astype(F32)
    causal = lax.broadcasted_iota(I32, (BLK, BLK), 0) <= lax.broadcasted_iota(I32, (BLK, BLK), 1)

    heads = (slice(0, dh), slice(dh, 2 * dh))
    qs = [qt_ref[0, hs, :] * jnp.asarray(dh ** -0.5, BF16) for hs in heads]

    def scores(j, h):
        c0 = pl.multiple_of(j * BLK, BLK)
        return _dot(k_ref[0, pl.ds(c0, BLK), heads[h]], qs[h])

    def values(j, h, p):
        c0 = pl.multiple_of(j * BLK, BLK)
        return _dot(vt_ref[0, heads[h], pl.ds(c0, BLK)], p)

    gates = []
    for h in range(2):
        km_hi, km_lo = _split(kmean_ref[:, heads[h]])
        gates.append(_dot(km_hi, qs[h]) + _dot(km_lo, qs[h]))
    s_own = [scores(i, h) for h in range(2)]
    s_first = [scores(0, h) for h in range(2)]

    start = []
    for h in range(2):
        gate = jnp.where(jf < i_f, gates[h], NEG)
        selm = jnp.zeros((nb, BLK), F32)
        for _ in range(top):
            m = jnp.max(gate, axis=0, keepdims=True)
            idx = jnp.min(jnp.where(gate == m, jf, float(nb)), axis=0, keepdims=True)
            hit = jf == idx
            selm = jnp.where(hit, jnp.where(m > 0.5 * NEG, 1.0, 0.0), selm)
            gate = jnp.where(hit, 2.0 * NEG, gate)
        sel_ref[h] = jnp.concatenate([selm, jnp.ones((SUBLANES, BLK), F32)], axis=0)
        sm = jnp.where(causal, s_own[h], NEG)
        m0 = jnp.max(sm, axis=0, keepdims=True)
        p = jnp.exp(sm - m0)
        s_ref[0, h] = s_first[h]
        p_ref[0, h] = p.astype(BF16)
        start += [m0, jnp.sum(p, axis=0, keepdims=True), jnp.zeros((dh, BLK), F32)]

    def prev_of(t):
        return jnp.where(t == 0, i, t - 1), jnp.where(t == 0, nb, t - 1)

    def stage(t, src, dst, carry):
        blk_prev, row_prev = prev_of(t)
        nxt = jnp.minimum(t + 1, nb - 1)
        for h in range(2):
            s_ref[dst, h] = scores(nxt, h)
        pv = [values(blk_prev, h, p_ref[src, h]) for h in range(2)]
        out = []
        for h in range(2):
            m_i, l_i, acc = carry[3 * h:3 * h + 3]
            on = sel_ref[h, pl.ds(t, 1), :] > 0.5
            on_prev = sel_ref[h, pl.ds(row_prev, 1), :] > 0.5
            m_new = jnp.where(on, jnp.maximum(m_i, jnp.max(s_ref[src, h], axis=0, keepdims=True)), m_i)
            p = jnp.exp(s_ref[src, h] - m_new)
            p_ref[dst, h] = p.astype(BF16)
            alpha = jnp.exp(m_i - m_new)
            l_new = alpha * l_i + jnp.where(on, jnp.sum(p, axis=0, keepdims=True), 0.0)
            out += [m_new, l_new, alpha * (acc + jnp.where(on_prev, pv[h], 0.0))]
        return tuple(out)

    n_pairs = (i + 1) // 2
    fin = lax.fori_loop(0, n_pairs, lambda u, c: stage(2 * u + 1, 1, 0, stage(2 * u, 0, 1, c)), tuple(start))
    blk_prev, row_prev = prev_of(2 * n_pairs)
    outs = []
    for h in range(2):
        on_prev = sel_ref[h, pl.ds(row_prev, 1), :] > 0.5
        pv = values(blk_prev, h, p_ref[0, h])
        outs.append((fin[3 * h + 2] + jnp.where(on_prev, pv, 0.0)) / fin[3 * h + 1])
    o_ref[0] = jnp.concatenate(outs, axis=0).astype(o_ref.dtype)


def _moba_attention(qt, k, vt, seq):
    B, width, _ = qt.shape
    BLK = MOBA_BLOCK
    nb = seq // BLK
    top = min(MOBA_TOPK, max(nb - 1, 1))
    return pl.pallas_call(
        functools.partial(_moba_kernel, seq=seq, top=top),
        grid=(B, width // LANES, nb),
        in_specs=[pl.BlockSpec((1, LANES, BLK), lambda b, h, i: (b, h, i)),
                  pl.BlockSpec((1, seq, LANES), lambda b, h, i: (b, 0, h)),
                  pl.BlockSpec((1, LANES, seq), lambda b, h, i: (b, h, 0))],
        out_specs=pl.BlockSpec((1, LANES, BLK), lambda b, h, i: (b, h, i)),
        out_shape=jax.ShapeDtypeStruct((B, width, seq), BF16),
        scratch_shapes=[pltpu.VMEM((nb, LANES), F32), pltpu.VMEM((2, nb + SUBLANES, BLK), F32),
                        pltpu.VMEM((2, 2, BLK, BLK), F32), pltpu.VMEM((2, 2, BLK, BLK), BF16)],
        compiler_params=_cparams("parallel", "parallel", "arbitrary"), name="moba_attn",
    )(qt, k, vt)


def _route(logits):
    n_fine = MOE_GROUPS * MOE_EXPERTS_PER_GROUP
    lane = lax.broadcasted_iota(I32, logits.shape, 1)
    lf = lane.astype(F32)
    is_coarse = (lane >= n_fine) & (lane < n_fine + MOE_GROUPS)
    pc = _masked_softmax(logits, is_coarse)
    gv = jnp.max(pc, axis=-1, keepdims=True)
    gidx = jnp.min(jnp.where(is_coarse & (pc == gv), lf - float(n_fine), float(MOE_GROUPS)), axis=-1, keepdims=True)
    in_group = (lane < n_fine) & ((lane // MOE_EXPERTS_PER_GROUP).astype(F32) == gidx)
    pf = _masked_softmax(logits, in_group)
    cand = jnp.where(in_group, pf, -1.0)
    m1 = jnp.max(cand, axis=-1, keepdims=True)
    i1 = jnp.min(jnp.where(cand == m1, lf, float(LANES)), axis=-1, keepdims=True)
    cand = jnp.where(lf == i1, -1.0, cand)
    m2 = jnp.max(cand, axis=-1, keepdims=True)
    i2 = jnp.min(jnp.where(cand == m2, lf, float(LANES)), axis=-1, keepdims=True)
    tot = m1 + m2
    return jnp.where(lf == i1, gv * (m1 / tot), jnp.where(lf == i2, gv * (m2 / tot), 0.0))


def _moe_kernel(x_ref, wr_ref, br_ref, wg_ref, wu_ref, wd_ref, lg_ref, lb_ref, o_ref, gates_ref, acc_ref, *, alpha):
    E, FF = MOE_EXPERTS_PER_GROUP, MOE_FF
    g = pl.program_id(1)
    x = x_ref[...]

    @pl.when(g == 0)
    def _():
        gates_ref[...] = _route(_dot3(x, wr_ref[...]) + br_ref[...])
        acc_ref[...] = jnp.zeros_like(acc_ref)

    xb = x.astype(BF16)
    hg = _dot(xb, wg_ref[0])
    hu = _dot(xb, wu_ref[0])
    h = hg * _sigmoid(hg) * hu
    gates = gates_ref[...]
    lane = lax.broadcasted_iota(I32, gates.shape, 1)
    parts = []
    for e in range(E):
        ge = jnp.sum(jnp.where(lane == g * E + e, gates, 0.0), axis=-1, keepdims=True)
        parts.append((h[:, e * FF:(e + 1) * FF] * ge).astype(BF16))
    acc_ref[...] += _dot(jnp.concatenate(parts, axis=-1), wd_ref[0])

    @pl.when(g == pl.num_programs(1) - 1)
    def _():
        o_ref[...] = _layer_norm(alpha * x + acc_ref[...], lg_ref[...], lb_ref[...])


def _moe(x2d, wr, br, wg, wu, wd, ln_g, ln_b, alpha):
    T, D = x2d.shape
    tm = min(ROW_TILE, T)
    NG = wg.shape[0]
    whole = lambda a: pl.BlockSpec(a.shape, lambda i, g: (0, 0))
    per_g = lambda a: pl.BlockSpec((1,) + a.shape[1:], lambda i, g: (g, 0, 0))
    return pl.pallas_call(
        functools.partial(_moe_kernel, alpha=alpha),
        grid=(T // tm, NG),
        in_specs=[pl.BlockSpec((tm, D), lambda i, g: (i, 0)), whole(wr), whole(br), per_g(wg), per_g(wu), per_g(wd),
                  whole(ln_g), whole(ln_b)],
        out_specs=pl.BlockSpec((tm, D), lambda i, g: (i, 0)),
        out_shape=jax.ShapeDtypeStruct((T, D), F32),
        scratch_shapes=[pltpu.VMEM((tm, LANES), F32), pltpu.VMEM((tm, D), F32)],
        compiler_params=_cparams("parallel", "arbitrary"), name="moe",
    )(x2d, wr, br, wg, wu, wd, ln_g, ln_b)


def _rope_tables(seq):
    inv = 1.0 / (ROPE_THETA ** (jnp.arange(0, HEAD_DIM, 2, dtype=F32) / HEAD_DIM))
    ang = jnp.arange(seq, dtype=F32)[:, None] * inv[None, :]
    cos, sin = jnp.cos(ang), jnp.sin(ang)
    reps = LANES // HEAD_DIM
    return jnp.tile(jnp.concatenate([cos, cos], -1), (1, reps)), jnp.tile(jnp.concatenate([-sin, sin], -1), (1, reps))


def _selection_constants(seq):
    ncp = seq // CMP_STRIDE
    nbs = seq // SEL_BLOCK
    starts = jnp.arange(ncp) * CMP_STRIDE
    bstart = jnp.arange(nbs) * SEL_BLOCK
    ovl = jnp.clip(jnp.minimum(starts[:, None] + CMP_BLOCK, bstart[None, :] + SEL_BLOCK)
                   - jnp.maximum(starts[:, None], bstart[None, :]), 0, CMP_BLOCK)
    return (ovl.astype(F32) / CMP_BLOCK).astype(BF16).T


def _moe_weights(w_coarse, b_coarse, w_fine, b_fine, w_gate, w_up, w_down):
    D = w_coarse.shape[0]
    NG, E, _, FF = w_gate.shape
    n_fine = NG * E
    wr = jnp.zeros((D, LANES), F32)
    wr = wr.at[:, :n_fine].set(w_fine.transpose(1, 0, 2).reshape(D, n_fine)).at[:, n_fine:n_fine + NG].set(w_coarse)
    br = jnp.zeros((1, LANES), F32)
    br = br.at[0, :n_fine].set(b_fine.reshape(n_fine)).at[0, n_fine:n_fine + NG].set(b_coarse)
    wg = w_gate.transpose(0, 2, 1, 3).reshape(NG, D, E * FF).astype(BF16)
    wu = w_up.transpose(0, 2, 1, 3).reshape(NG, D, E * FF).astype(BF16)
    wd = w_down.reshape(NG, E * FF, D).astype(BF16)
    return wr, br, wg, wu, wd


def _even_mixer(x2d, B, S, cos2, sin2, w_in, pe_k, pe_v, k_w1, k_w2, v_w1, v_w2,
                lam_re, lam_im, log_step, b_re, b_im, c_re, c_im, d_skip, glu_w, glu_b, w_out, ln_g, ln_b, alpha):
    G, Z, dh = NSA_KV_GROUPS, NSA_HEADS_PER_GROUP, HEAD_DIM
    kvw = NSA_KV_WIDTH
    D = x2d.shape[1]
    cuts = [NSA_WIDTH + i * kvw for i in range(7)]
    w_q, w_kc, w_vc, w_ks, w_vs, w_kw, w_vw = [w_in[:, a:b] for a, b in zip([0] + cuts[:-1], cuts)]
    n_gate = 3 * NSA_HEADS
    w_g = jnp.pad(w_in[:, cuts[-1]:cuts[-1] + n_gate], ((0, 0), (0, LANES - n_gate)))
    w_u = w_in[:, cuts[-1] + n_gate:]
    weights = [jnp.concatenate([w_q, w_kc, w_ks, w_kw], 1).astype(BF16),
               jnp.concatenate([w_vc, w_vs, w_vw], 1).astype(BF16), w_u.astype(BF16), w_g.astype(BF16)]
    qk, vv, u, gates = _project(x2d, cos2, sin2, weights, ["rope", "plain", "plain", "sigmoid"],
                                [BF16, BF16, F32, F32], S)

    def heads(t):
        return t.reshape(B, S, G, dh).transpose(0, 2, 1, 3)

    QC = NSA_QCHUNK
    qt = qk[:, :NSA_WIDTH].reshape(B, S // QC, QC, G, Z, dh).transpose(0, 3, 1, 5, 4, 2).reshape(B, G, S // QC, dh, Z * QC)
    kc, ks, kw = [heads(qk[:, NSA_WIDTH + i * kvw:NSA_WIDTH + (i + 1) * kvw]) for i in range(3)]
    vc, vs, vw = [heads(vv[:, i * kvw:(i + 1) * kvw]) for i in range(3)]
    gates_t = gates[:, :n_gate].reshape(B, S, G, 3 * Z).transpose(0, 2, 3, 1)

    n_rows = S // CMP_STRIDE
    r = jnp.stack([kc, vc]).reshape(2, B * G, n_rows, CMP_STRIDE * dh)
    pe = jnp.stack([pe_k, pe_v]).reshape(2, 1, CMP_BLOCK * dh)
    w1 = jnp.stack([k_w1, v_w1]).astype(BF16)
    w2 = jnp.stack([k_w2, v_w2]).astype(BF16)
    cmp, cmp_t = _nsa_compress(r, pe, w1, w2, w2.transpose(0, 2, 1))
    kcm = cmp[0].reshape(B, G, n_rows, dh)
    vct = cmp_t[1].reshape(B, G, dh, n_rows)

    o_t = _nsa_attention(qt, kcm, vct, ks, vs.transpose(0, 1, 3, 2), kw, vw.transpose(0, 1, 3, 2), gates_t,
                         _selection_constants(S), S)
    o_nsa = o_t.reshape(B, G, S // QC, dh, Z, QC).transpose(0, 2, 5, 1, 4, 3)

    C, L = S5_GROUP_CH, S5_CHUNK
    SG = u.shape[1] // C
    mt, pm, qt, al = _s5_operators(lam_re[:, None, :], lam_im[:, None, :], log_step[:, None, None],
                                   b_re.transpose(0, 2, 1), b_im.transpose(0, 2, 1), c_re, c_im)
    u_g = u.reshape(B, S // L, L, SG, C).transpose(3, 1, 0, 2, 4).reshape(SG, (S // L) * B, L * C)
    d_t = jnp.tile(d_skip.reshape(SG, 1, C), (1, 1, L))
    y = _s5_scan(u_g, mt, pm, qt, al, d_t, B)
    y = y.reshape(SG, S // L, B, L, C).transpose(2, 1, 3, 0, 4).reshape(B * S, SG * C)

    return _even_out(x2d, o_nsa.reshape(B * S, NSA_WIDTH), y, glu_w.astype(BF16), glu_b[None, :],
                     w_out[:NSA_WIDTH].astype(BF16), w_out[NSA_WIDTH:].astype(BF16), ln_g[None, :], ln_b[None, :], alpha)


def _odd_mixer(x2d, B, S, cos2, sin2, w_in, w_out, ln_g, ln_b, alpha):
    D = x2d.shape[1]
    qk, v = _project(x2d, cos2, sin2, [w_in[:, :2 * D].astype(BF16), w_in[:, 2 * D:].astype(BF16)],
                     ["rope", "plain"], [BF16, BF16], S)
    qt = qk[:, :D].reshape(B, S, D).transpose(0, 2, 1)
    k = qk[:, D:].reshape(B, S, D)
    vt = v.reshape(B, S, D).transpose(0, 2, 1)
    attn = _moba_attention(qt, k, vt, S).transpose(0, 2, 1)
    return _odd_out(x2d, attn.reshape(B * S, D), w_out.astype(BF16), ln_g[None, :], ln_b[None, :], alpha)


def kernel(x, ev_w_in, nsa_pe_k, nsa_pe_v, nsa_cmp_k_w1, nsa_cmp_k_w2, nsa_cmp_v_w1, nsa_cmp_v_w2, s5_lambda_re, s5_lambda_im, s5_log_step, s5_b_re, s5_b_im, s5_c_re, s5_c_im, s5_d, s5_glu_w, s5_glu_b, ev_w_out, od_w_in, od_w_out, ln_mix_g, ln_mix_b, ln_ffn_g, ln_ffn_b, moe_w_coarse, moe_b_coarse, moe_w_fine, moe_b_fine, moe_w_gate, moe_w_up, moe_w_down):
    B, S, D = x.shape
    depth = ln_mix_g.shape[0]
    alpha = (2.0 * depth) ** 0.25
    cos2, sin2 = _rope_tables(S)
    h = x.reshape(B * S, D)
    for layer in range(depth):
        if layer % 2 == 0:
            e = layer // 2
            h = _even_mixer(h, B, S, cos2, sin2, ev_w_in[e], nsa_pe_k[e], nsa_pe_v[e], nsa_cmp_k_w1[e], nsa_cmp_k_w2[e],
                            nsa_cmp_v_w1[e], nsa_cmp_v_w2[e], s5_lambda_re[e], s5_lambda_im[e], s5_log_step[e],
                            s5_b_re[e], s5_b_im[e], s5_c_re[e], s5_c_im[e], s5_d[e], s5_glu_w[e], s5_glu_b[e],
                            ev_w_out[e], ln_mix_g[layer], ln_mix_b[layer], alpha)
        else:
            o = layer // 2
            h = _odd_mixer(h, B, S, cos2, sin2, od_w_in[o], od_w_out[o], ln_mix_g[layer], ln_mix_b[layer], alpha)
        wr, br, wg, wu, wd = _moe_weights(moe_w_coarse[layer], moe_b_coarse[layer], moe_w_fine[layer], moe_b_fine[layer],
                                          moe_w_gate[layer], moe_w_up[layer], moe_w_down[layer])
        h = _moe(h, wr, br, wg, wu, wd, ln_ffn_g[layer][None, :], ln_ffn_b[layer][None, :], alpha)
    return h.reshape(B, S, D)
```

```python
import functools
import math

import jax
import jax.numpy as jnp
from jax import lax
from jax.experimental import pallas as pl
from jax.experimental.pallas import tpu as pltpu

F32 = jnp.float32
BF16 = jnp.bfloat16
I32 = jnp.int32

HEAD_DIM = 64
ROPE_THETA = 10000.0
LN_EPS = 1e-5

NSA_KV_GROUPS = 2
NSA_HEADS_PER_GROUP = 4
NSA_HEADS = NSA_KV_GROUPS * NSA_HEADS_PER_GROUP
NSA_WIDTH = NSA_HEADS * HEAD_DIM
NSA_KV_WIDTH = NSA_KV_GROUPS * HEAD_DIM
CMP_BLOCK = 32
CMP_STRIDE = 16
SEL_BLOCK = 64
SEL_TOPN = 8
WINDOW = 256
NSA_QCHUNK = 128
NSA_KV_TILE = 512

S5_GROUP_CH = 16
S5_STATE = 64
S5_CHUNK = 64

MOBA_BLOCK = 256
MOBA_TOPK = 3

MOE_GROUPS = 4
MOE_EXPERTS_PER_GROUP = 8
MOE_FF = 128

LANES = 128
SUBLANES = 8
MXU_COLS = 256
ROW_TILE = 512
VMEM_LIMIT = 56 * 1024 * 1024
NEG = -1e30


def _cparams(*sem):
    return pltpu.CompilerParams(dimension_semantics=sem, vmem_limit_bytes=VMEM_LIMIT)


def _dot(a, b):
    return jnp.dot(a, b, preferred_element_type=F32)


def _dot_nt(a, b):
    return lax.dot_general(a, b, (((1,), (1,)), ((), ())), preferred_element_type=F32)


def _split(a):
    hi = a.astype(BF16)
    lo = (a - hi.astype(F32)).astype(BF16)
    return hi, lo


def _dot3(a, b):
    ah, al = _split(a)
    bh, bl = _split(b)
    return _dot(ah, bh) + _dot(ah, bl) + _dot(al, bh)


def _dot3_nt(a, b):
    ah, al = _split(a)
    bh, bl = _split(b)
    return _dot_nt(ah, bh) + _dot_nt(ah, bl) + _dot_nt(al, bh)


def _sigmoid(x):
    return 1.0 / (1.0 + jnp.exp(-x))


def _gelu_tanh(x):
    return 0.5 * x * (1.0 + jnp.tanh(math.sqrt(2.0 / math.pi) * (x + 0.044715 * (x * x * x))))


def _layer_norm(r, g, b):
    mu = jnp.mean(r, axis=-1, keepdims=True)
    d = r - mu
    var = jnp.mean(d * d, axis=-1, keepdims=True)
    return d * lax.rsqrt(var + LN_EPS) * g + b


def _masked_softmax(s, ok):
    sm = jnp.where(ok, s, NEG)
    m = jnp.max(sm, axis=-1, keepdims=True)
    p = jnp.where(ok, jnp.exp(sm - m), 0.0)
    return p / jnp.maximum(jnp.sum(p, axis=-1, keepdims=True), 1e-30)


def _rope_128(a, cos2, sin2):
    lane = lax.broadcasted_iota(I32, a.shape, 1)
    first_half = (lane & (HEAD_DIM - 1)) < (HEAD_DIM // 2)
    partner = jnp.where(first_half, pltpu.roll(a, LANES - HEAD_DIM // 2, 1), pltpu.roll(a, HEAD_DIM // 2, 1))
    return a * cos2 + partner * sin2


def _proj_kernel(x_ref, cos_ref, sin_ref, *refs, kinds):
    n = len(kinds)
    w_refs, o_refs = refs[:n], refs[n:]
    xb = x_ref[...].astype(BF16)
    cos2 = cos_ref[...]
    sin2 = sin_ref[...]
    for kind, w_ref, o_ref in zip(kinds, w_refs, o_refs):
        width = w_ref.shape[1]
        for c0 in range(0, width, MXU_COLS):
            cw = min(MXU_COLS, width - c0)
            acc = _dot(xb, w_ref[:, c0:c0 + cw])
            for l0 in range(0, cw, LANES):
                part = acc[:, l0:l0 + LANES]
                if kind == "rope":
                    part = _rope_128(part, cos2, sin2)
                elif kind == "sigmoid":
                    part = _sigmoid(part)
                o_ref[:, c0 + l0:c0 + l0 + LANES] = part.astype(o_ref.dtype)


def _project(x2d, cos2, sin2, weights, kinds, dtypes, seq):
    T, D = x2d.shape
    tm = min(ROW_TILE, seq)
    pos_tiles = seq // tm
    in_specs = [pl.BlockSpec((tm, D), lambda i: (i, 0)),
                pl.BlockSpec((tm, LANES), lambda i: (i % pos_tiles, 0)),
                pl.BlockSpec((tm, LANES), lambda i: (i % pos_tiles, 0))]
    in_specs += [pl.BlockSpec(w.shape, lambda i: (0, 0)) for w in weights]
    out_specs = [pl.BlockSpec((tm, w.shape[1]), lambda i: (i, 0)) for w in weights]
    out_shape = [jax.ShapeDtypeStruct((T, w.shape[1]), dt) for w, dt in zip(weights, dtypes)]
    return pl.pallas_call(
        functools.partial(_proj_kernel, kinds=tuple(kinds)),
        grid=(T // tm,), in_specs=in_specs, out_specs=out_specs, out_shape=out_shape,
        compiler_params=_cparams("parallel"), name="proj",
    )(x2d, cos2, sin2, *weights)


def _compress_kernel(r_ref, pe_ref, w1_ref, w2_ref, w2t_ref, o_ref, ot_ref):
    half = CMP_STRIDE * HEAD_DIM
    r = r_ref[0, 0]
    n = r.shape[0]
    top = _dot(r, w1_ref[0, :half, :])
    bot = _dot(r, w1_ref[0, half:, :])
    bias = _dot(pe_ref[0].astype(BF16), w1_ref[0])
    h = top + pltpu.roll(bot, n - 1, 0) + bias
    hb = _gelu_tanh(h).astype(BF16)
    o_ref[0, 0] = _dot(hb, w2_ref[0]).astype(o_ref.dtype)
    ot_ref[0, 0] = _dot_nt(w2t_ref[0], hb).astype(ot_ref.dtype)


def _nsa_compress(r, pe, w1, w2, w2t):
    _, bg, n, width = r.shape
    hid = w1.shape[2]
    return pl.pallas_call(
        _compress_kernel,
        grid=(2, bg),
        in_specs=[pl.BlockSpec((1, 1, n, width), lambda a, i: (a, i, 0, 0)),
                  pl.BlockSpec((1, 1, 2 * width), lambda a, i: (a, 0, 0)),
                  pl.BlockSpec((1, 2 * width, hid), lambda a, i: (a, 0, 0)),
                  pl.BlockSpec((1, hid, HEAD_DIM), lambda a, i: (a, 0, 0)),
                  pl.BlockSpec((1, HEAD_DIM, hid), lambda a, i: (a, 0, 0))],
        out_specs=[pl.BlockSpec((1, 1, n, HEAD_DIM), lambda a, i: (a, i, 0, 0)),
                   pl.BlockSpec((1, 1, HEAD_DIM, n), lambda a, i: (a, i, 0, 0))],
        out_shape=[jax.ShapeDtypeStruct((2, bg, n, HEAD_DIM), BF16),
                   jax.ShapeDtypeStruct((2, bg, HEAD_DIM, n), BF16)],
        compiler_params=_cparams("parallel", "parallel"), name="nsa_compress",
    )(r, pe, w1, w2, w2t)


def _exp_cols(s, ok):
    sm = jnp.where(ok, s, NEG)
    m = jnp.max(sm, axis=0, keepdims=True)
    p = jnp.exp(sm - jnp.where(m > 0.5 * NEG, m, 0.0))
    return p, jnp.sum(p, axis=0, keepdims=True)


def _nsa_attn_kernel(q_ref, kc_ref, vct_ref, ks_ref, vs_ref, kw_ref, vw_ref, gt_ref, c2st_ref, o_ref,
                     vst_ref, vwt_ref, sel_ref, s_ref, p_ref, *, seq, n_sel):
    Z, dh, QC, SB = NSA_HEADS_PER_GROUP, HEAD_DIM, NSA_QCHUNK, SEL_BLOCK
    W = Z * QC
    KT = 2 * SB
    ncp = seq // CMP_STRIDE
    nbs = seq // SEL_BLOCK
    c = pl.program_id(2)
    t0 = c * QC

    @pl.when(c == 0)
    def _():
        step = 4 * QC
        for r in range(seq // step):
            rows = slice(r * step, (r + 1) * step)
            vst_ref[:, rows] = _transpose_bf16(vs_ref[0, 0, rows, :]).astype(BF16)
            vwt_ref[:, rows] = _transpose_bf16(vw_ref[0, 0, rows, :]).astype(BF16)

    q_t = _transpose_bf16(q_ref[0]).astype(BF16)
    qt = jnp.concatenate([q_t[z * dh:(z + 1) * dh] for z in range(Z)], axis=1)
    qt = qt * jnp.asarray(dh ** -0.5, BF16)
    qpos = t0 + (lax.broadcasted_iota(I32, (1, W), 1) & (QC - 1))
    d0 = pl.multiple_of(t0, QC)
    band = QC + WINDOW
    w0 = pl.multiple_of(jnp.maximum(t0 - WINDOW, 0), QC)

    def tile4(row):
        return jnp.concatenate([row] * Z, axis=1)

    s_c = _dot(kc_ref[0, 0], qt)
    s_w = _dot(kw_ref[0, 0, pl.ds(w0, band), :], qt)
    s_d = _dot(ks_ref[0, 0, pl.ds(d0, KT), :], qt)
    s_ref[0] = _dot(ks_ref[0, 0, pl.ds(0, KT), :], qt)

    cmp_end = lax.broadcasted_iota(I32, (ncp, 1), 0) * CMP_STRIDE + (CMP_BLOCK - 1)
    e_c, l_c = _exp_cols(s_c, cmp_end <= qpos)
    p_c = e_c * (1.0 / jnp.maximum(l_c, 1e-30))
    o_c = _dot(vct_ref[0, 0], p_c.astype(BF16))

    p_sum = p_c[:, 0:QC]
    for z in range(1, Z):
        p_sum = p_sum + p_c[:, z * QC:(z + 1) * QC]
    p_hi, p_lo = _split(p_sum)
    imp = _dot(c2st_ref[...], p_hi) + _dot(c2st_ref[...], p_lo)

    jf = lax.broadcasted_iota(I32, (nbs, QC), 0).astype(F32)
    qblk = (qpos[:, :QC] // SEL_BLOCK).astype(F32)
    future = jf > qblk
    forced = (jf == 0.0) | (jf == qblk) | (jf == qblk - 1.0)
    score = jnp.where(future, -1.0, jnp.where(forced, 1e3, imp))
    selm = jnp.zeros((nbs, QC), F32)
    for _ in range(n_sel):
        m = jnp.max(score, axis=0, keepdims=True)
        idx = jnp.min(jnp.where(score == m, jf, float(nbs)), axis=0, keepdims=True)
        hit = jf == idx
        selm = jnp.where(hit, jnp.where(m >= 0.0, 1.0, 0.0), selm)
        score = jnp.where(hit, NEG, score)
    selw = tile4(selm)
    sel_ref[...] = jnp.where(lax.broadcasted_iota(I32, (nbs, W), 0) < 2 * c, selw, 0.0)

    brow = lax.broadcasted_iota(I32, (nbs, 1), 0)
    on_a = jnp.sum(jnp.where(brow == 2 * c, selw, 0.0), axis=0, keepdims=True)
    on_b = jnp.sum(jnp.where(brow == 2 * c + 1, selw, 0.0), axis=0, keepdims=True)
    krow = lax.broadcasted_iota(I32, (KT, 1), 0)
    ok_d = jnp.where(t0 + krow <= qpos, jnp.where(krow < SB, on_a, on_b), 0.0) > 0.5
    sm = jnp.where(ok_d, s_d, NEG)
    m0 = jnp.max(sm, axis=0, keepdims=True)
    p_d = jnp.exp(sm - m0)
    l0 = jnp.sum(p_d, axis=0, keepdims=True)
    acc0 = _dot(vst_ref[:, pl.ds(d0, KT)], p_d.astype(BF16))
    p_ref[0] = jnp.zeros((KT, W), BF16)

    dist = qpos - (w0 + lax.broadcasted_iota(I32, (band, 1), 0))
    e_w, l_w = _exp_cols(s_w, jnp.where(dist >= 0, dist, WINDOW) < WINDOW)
    o_w = _dot(vwt_ref[:, pl.ds(w0, band)], e_w.astype(BF16)) * (1.0 / jnp.maximum(l_w, 1e-30))

    def stage(t, src, dst, carry):
        m_i, l_i, acc = carry
        tp = jnp.maximum(t - 1, 0)
        k_next = pl.multiple_of(jnp.minimum(t + 1, c) * KT, KT)
        s_ref[dst] = _dot(ks_ref[0, 0, pl.ds(k_next, KT), :], qt)
        vt = vst_ref[:, pl.ds(pl.multiple_of(tp * KT, KT), KT)]
        pv_a = _dot(vt[:, :SB], p_ref[src, 0:SB, :])
        pv_b = _dot(vt[:, SB:], p_ref[src, SB:KT, :])
        on_a = sel_ref[pl.ds(2 * t, 1), :] > 0.5
        on_b = sel_ref[pl.ds(2 * t + 1, 1), :] > 0.5
        prev_a = sel_ref[pl.ds(2 * tp, 1), :] > 0.5
        prev_b = sel_ref[pl.ds(2 * tp + 1, 1), :] > 0.5
        mx_a = jnp.max(s_ref[src, 0:SB, :], axis=0, keepdims=True)
        mx_b = jnp.max(s_ref[src, SB:KT, :], axis=0, keepdims=True)
        m_new = jnp.maximum(m_i, jnp.maximum(jnp.where(on_a, mx_a, NEG), jnp.where(on_b, mx_b, NEG)))
        p = jnp.exp(s_ref[src] - m_new)
        p_ref[dst] = p.astype(BF16)
        alpha = jnp.exp(m_i - m_new)
        l_new = (alpha * l_i + jnp.where(on_a, jnp.sum(p[:SB], axis=0, keepdims=True), 0.0)
                 + jnp.where(on_b, jnp.sum(p[SB:], axis=0, keepdims=True), 0.0))
        acc_new = alpha * (acc + jnp.where(prev_a, pv_a, 0.0) + jnp.where(prev_b, pv_b, 0.0))
        return m_new, l_new, acc_new

    n_pairs = (c + 1) // 2
    _, l_s, acc_s = lax.fori_loop(0, n_pairs, lambda u, cr: stage(2 * u + 1, 1, 0, stage(2 * u, 0, 1, cr)),
                                  (m0, l0, acc0))
    tl = jnp.maximum(2 * n_pairs - 1, 0)
    vt = vst_ref[:, pl.ds(pl.multiple_of(tl * KT, KT), KT)]
    last_a = sel_ref[pl.ds(2 * tl, 1), :] > 0.5
    last_b = sel_ref[pl.ds(2 * tl + 1, 1), :] > 0.5
    acc_s = (acc_s + jnp.where(last_a, _dot(vt[:, :SB], p_ref[0, 0:SB, :]), 0.0)
             + jnp.where(last_b, _dot(vt[:, SB:], p_ref[0, SB:KT, :]), 0.0))
    o_s = acc_s / jnp.maximum(l_s, 1e-30)

    gt = gt_ref[0, 0]
    g_c = jnp.concatenate([gt[3 * z:3 * z + 1] for z in range(Z)], axis=1)
    g_s = jnp.concatenate([gt[3 * z + 1:3 * z + 2] for z in range(Z)], axis=1)
    g_w = jnp.concatenate([gt[3 * z + 2:3 * z + 3] for z in range(Z)], axis=1)
    o_t = (g_c * o_c + g_s * o_s + g_w * o_w).astype(BF16)
    o_ref[0] = jnp.concatenate([_transpose_bf16(o_t[:, z * QC:(z + 1) * QC]) for z in range(Z)],
                               axis=1).astype(o_ref.dtype)


def _nsa_attention(q, kc, vct, ks, vs, kw, vw, gates_t, c2st, seq):
    B, G, _, dh = ks.shape
    Z, QC = NSA_HEADS_PER_GROUP, NSA_QCHUNK
    W = Z * QC
    KT = 2 * SEL_BLOCK
    ncp = seq // CMP_STRIDE
    nbs = seq // SEL_BLOCK
    full = lambda shp: pl.BlockSpec((1, 1) + shp, lambda b, g, c: (b, g, 0, 0))
    return pl.pallas_call(
        functools.partial(_nsa_attn_kernel, seq=seq, n_sel=min(SEL_TOPN, nbs)),
        grid=(B, G, seq // QC),
        in_specs=[pl.BlockSpec((1, QC, Z * dh), lambda b, g, c: (b, c, g)),
                  full((ncp, dh)), full((dh, ncp)), full((seq, dh)), full((seq, dh)), full((seq, dh)), full((seq, dh)),
                  pl.BlockSpec((1, 1, gates_t.shape[2], QC), lambda b, g, c: (b, g, 0, c)),
                  pl.BlockSpec((nbs, ncp), lambda b, g, c: (0, 0))],
        out_specs=pl.BlockSpec((1, QC, Z * dh), lambda b, g, c: (b, c, g)),
        out_shape=jax.ShapeDtypeStruct((B, seq, G * Z * dh), BF16),
        scratch_shapes=[pltpu.VMEM((dh, seq), BF16), pltpu.VMEM((dh, seq), BF16),
                        pltpu.VMEM((nbs, W), F32), pltpu.VMEM((2, KT, W), F32), pltpu.VMEM((2, KT, W), BF16)],
        compiler_params=_cparams("parallel", "parallel", "arbitrary"), name="nsa_attn",
    )(q, kc, vct, ks, vs, kw, vw, gates_t, c2st)


def _cmul(ar, ai, br, bi):
    return ar * br - ai * bi, ar * bi + ai * br


def _s5_ops_kernel(lre_ref, lim_ref, lstep_ref, bret_ref, bimt_ref, cre_ref, cim_ref,
                   mt_ref, pm_ref, qt_ref, al_ref):
    L, C, P = S5_CHUNK, S5_GROUP_CH, S5_STATE
    lr = lre_ref[0]
    li = lim_ref[0]
    step = jnp.exp(lstep_ref[0])
    mag = jnp.exp(lr * step)
    ar = mag * jnp.cos(li * step)
    ai = mag * jnp.sin(li * step)
    den = lr * lr + li * li
    fr = ((ar - 1.0) * lr + ai * li) / den
    fi = (ai * lr - (ar - 1.0) * li) / den
    bbr, bbi = _cmul(fr, fi, bret_ref[0], bimt_ref[0])

    up_r, up_i = jnp.ones((1, 1, P), F32), jnp.zeros((1, 1, P), F32)
    dn_r, dn_i = up_r, up_i
    sr, si = ar.reshape(1, 1, P), ai.reshape(1, 1, P)
    a1r, a1i = sr, si
    n = 1
    while n < L:
        tr, ti = _cmul(up_r, up_i, sr, si)
        up_r, up_i = jnp.concatenate([up_r, tr], 0), jnp.concatenate([up_i, ti], 0)
        tr, ti = _cmul(dn_r, dn_i, sr, si)
        dn_r, dn_i = jnp.concatenate([tr, dn_r], 0), jnp.concatenate([ti, dn_i], 0)
        sr, si = _cmul(sr, si, sr, si)
        n *= 2
    al_ref[0] = jnp.concatenate([sr.reshape(1, P), si.reshape(1, P)], -1)

    cr = cre_ref[0][None]
    ci = cim_ref[0][None]
    nr, ni = _cmul(up_r, up_i, a1r, a1i)
    wr, wi = _cmul(cr, ci, nr, ni)
    qt_ref[0] = jnp.concatenate([wr.reshape(L * C, P), -wi.reshape(L * C, P)], -1)
    er, ei = _cmul(dn_r, dn_i, bbr[None], bbi[None])
    pm_ref[0] = jnp.concatenate([er.reshape(L * C, P), ei.reshape(L * C, P)], -1).astype(pm_ref.dtype)
    w0r, w0i = _cmul(cr, ci, up_r, up_i)
    kt = _dot3_nt(bbr, w0r.reshape(L * C, P)) - _dot3_nt(bbi, w0i.reshape(L * C, P))
    lane = lax.broadcasted_iota(I32, kt.shape, 1)
    mt_ref[0, 0:C, :] = kt.astype(mt_ref.dtype)
    for s in range(1, L):
        shifted = jnp.where(lane >= s * C, pltpu.roll(kt, s * C, 1), 0.0)
        mt_ref[0, s * C:(s + 1) * C, :] = shifted.astype(mt_ref.dtype)


def _s5_operators(lre, lim, lstep, bret, bimt, cre, cim):
    G = lre.shape[0]
    L, C, P = S5_CHUNK, S5_GROUP_CH, S5_STATE
    vec = pl.BlockSpec((1, 1, P), lambda g: (g, 0, 0))
    mat = pl.BlockSpec((1, C, P), lambda g: (g, 0, 0))
    return pl.pallas_call(
        _s5_ops_kernel,
        grid=(G,),
        in_specs=[vec, vec, pl.BlockSpec((1, 1, 1), lambda g: (g, 0, 0)), mat, mat, mat, mat],
        out_specs=[pl.BlockSpec((1, L * C, L * C), lambda g: (g, 0, 0)),
                   pl.BlockSpec((1, L * C, 2 * P), lambda g: (g, 0, 0)),
                   pl.BlockSpec((1, L * C, 2 * P), lambda g: (g, 0, 0)),
                   pl.BlockSpec((1, 1, 2 * P), lambda g: (g, 0, 0))],
        out_shape=[jax.ShapeDtypeStruct((G, L * C, L * C), BF16),
                   jax.ShapeDtypeStruct((G, L * C, 2 * P), BF16),
                   jax.ShapeDtypeStruct((G, L * C, 2 * P), F32),
                   jax.ShapeDtypeStruct((G, 1, 2 * P), F32)],
        compiler_params=_cparams("parallel"), name="s5_operators",
    )(lre, lim, lstep, bret, bimt, cre, cim)


def _s5_scan_kernel(u_ref, mt_ref, pm_ref, qt_ref, al_ref, d_ref, y_ref, xin_ref, *, batch):
    P = S5_STATE
    rows = u_ref.shape[1]
    u = u_ref[0]
    ub = u.astype(BF16)
    y_ref[0] = _dot(ub, mt_ref[0]) + d_ref[0] * u
    xloc = _dot(ub, pm_ref[0])
    al = al_ref[0]
    lane = lax.broadcasted_iota(I32, (1, 2 * P), 1)
    mul_same = jnp.where(lane < P, al, pltpu.roll(al, P, 1))
    mul_swap = jnp.where(lane < P, -pltpu.roll(al, P, 1), al)
    xin_ref[...] = xloc

    def carry_step(k, state):
        r0 = pl.multiple_of(k * batch, batch)
        loc = xin_ref[pl.ds(r0, batch), :]
        xin_ref[pl.ds(r0, batch), :] = state
        return state * mul_same + pltpu.roll(state, P, 1) * mul_swap + loc

    lax.fori_loop(0, rows // batch, carry_step, jnp.zeros((batch, 2 * P), F32))
    y_ref[0] += _dot3_nt(xin_ref[...], qt_ref[0])


def _s5_scan(u, mt, pm, qt, al, d, batch):
    G, rows, width = u.shape
    P2 = 2 * S5_STATE
    per_g = lambda shp: pl.BlockSpec((1,) + shp, lambda g: (g, 0, 0))
    return pl.pallas_call(
        functools.partial(_s5_scan_kernel, batch=batch),
        grid=(G,),
        in_specs=[per_g((rows, width)), per_g((width, width)), per_g((width, P2)), per_g((width, P2)),
                  per_g((1, P2)), per_g((1, width))],
        out_specs=per_g((rows, width)),
        out_shape=jax.ShapeDtypeStruct((G, rows, width), F32),
        scratch_shapes=[pltpu.VMEM((rows, P2), F32)],
        compiler_params=_cparams("parallel"), name="s5_scan",
    )(u, mt, pm, qt, al, d)


def _even_out_kernel(x_ref, nsa_ref, y_ref, gw_ref, gb_ref, wa_ref, wb_ref, lg_ref, lb_ref, o_ref, *, alpha):
    g = _gelu_tanh(y_ref[...])
    s5 = g * _sigmoid(_dot(g.astype(BF16), gw_ref[...]) + gb_ref[...])
    mix = _dot(nsa_ref[...], wa_ref[...]) + _dot(s5.astype(BF16), wb_ref[...])
    o_ref[...] = _layer_norm(alpha * x_ref[...] + mix, lg_ref[...], lb_ref[...])


def _even_out(x2d, o_nsa, y_s5, glu_w, glu_b, wa, wb, ln_g, ln_b, alpha):
    T, D = x2d.shape
    tm = min(ROW_TILE, T)
    row = lambda w: pl.BlockSpec((tm, w), lambda i: (i, 0))
    whole = lambda a: pl.BlockSpec(a.shape, lambda i: (0, 0))
    return pl.pallas_call(
        functools.partial(_even_out_kernel, alpha=alpha),
        grid=(T // tm,),
        in_specs=[row(D), row(o_nsa.shape[1]), row(y_s5.shape[1]), whole(glu_w), whole(glu_b), whole(wa), whole(wb),
                  whole(ln_g), whole(ln_b)],
        out_specs=row(D), out_shape=jax.ShapeDtypeStruct((T, D), F32),
        compiler_params=_cparams("parallel"), name="even_out",
    )(x2d, o_nsa, y_s5, glu_w, glu_b, wa, wb, ln_g, ln_b)


def _odd_out_kernel(x_ref, a_ref, w_ref, lg_ref, lb_ref, o_ref, *, alpha):
    mix = _dot(a_ref[...], w_ref[...])
    o_ref[...] = _layer_norm(alpha * x_ref[...] + mix, lg_ref[...], lb_ref[...])


def _odd_out(x2d, attn, w, ln_g, ln_b, alpha):
    T, D = x2d.shape
    tm = min(ROW_TILE, T)
    row = lambda wd: pl.BlockSpec((tm, wd), lambda i: (i, 0))
    whole = lambda a: pl.BlockSpec(a.shape, lambda i: (0, 0))
    return pl.pallas_call(
        functools.partial(_odd_out_kernel, alpha=alpha),
        grid=(T // tm,),
        in_specs=[row(D), row(attn.shape[1]), whole(w), whole(ln_g), whole(ln_b)],
        out_specs=row(D), out_shape=jax.ShapeDtypeStruct((T, D), F32),
        compiler_params=_cparams("parallel"), name="odd_out",
    )(x2d, attn, w, ln_g, ln_b)


def _eye(n):
    return (lax.broadcasted_iota(I32, (n, n), 0) == lax.broadcasted_iota(I32, (n, n), 1)).astype(BF16)


def _transpose_bf16(a):
    return _dot_nt(_eye(a.shape[1]), a)


def _moba_kernel(q_ref, k_ref, v_ref, o_ref, kmean_ref, vt_ref, sel_ref, s_ref, p_ref, *, seq, top):
    BLK, dh = MOBA_BLOCK, HEAD_DIM
    nb = seq // BLK
    i = pl.program_id(2)

    @pl.when(i == 0)
    def _():
        kmean_ref[...] = jnp.mean(k_ref[0].astype(F32).reshape(nb, BLK, 2 * dh), axis=1)
        for r in range(nb):
            vt_ref[:, r * BLK:(r + 1) * BLK] = _transpose_bf16(v_ref[0, r * BLK:(r + 1) * BLK, :]).astype(BF16)

    jf = lax.broadcasted_iota(I32, (nb, BLK), 0).astype(F32)
    i_f = i.astype(F32)
    causal = lax.broadcasted_iota(I32, (BLK, BLK), 0) <= lax.broadcasted_iota(I32, (BLK, BLK), 1)

    heads = (slice(0, dh), slice(dh, 2 * dh))
    qt = _transpose_bf16(q_ref[0]).astype(BF16) * jnp.asarray(dh ** -0.5, BF16)
    qs = [qt[hs, :] for hs in heads]

    def scores(j, h):
        c0 = pl.multiple_of(j * BLK, BLK)
        return _dot(k_ref[0, pl.ds(c0, BLK), heads[h]], qs[h])

    def values(j, h, p):
        c0 = pl.multiple_of(j * BLK, BLK)
        return _dot(vt_ref[heads[h], pl.ds(c0, BLK)], p)

    gates = []
    for h in range(2):
        km_hi, km_lo = _split(kmean_ref[:, heads[h]])
        gates.append(_dot(km_hi, qs[h]) + _dot(km_lo, qs[h]))
    s_own = [scores(i, h) for h in range(2)]
    s_first = [scores(0, h) for h in range(2)]

    start = []
    for h in range(2):
        gate = jnp.where(jf < i_f, gates[h], NEG)
        selm = jnp.zeros((nb, BLK), F32)
        for _ in range(top):
            m = jnp.max(gate, axis=0, keepdims=True)
            idx = jnp.min(jnp.where(gate == m, jf, float(nb)), axis=0, keepdims=True)
            hit = jf == idx
            selm = jnp.where(hit, jnp.where(m > 0.5 * NEG, 1.0, 0.0), selm)
            gate = jnp.where(hit, 2.0 * NEG, gate)
        sel_ref[h] = jnp.concatenate([selm, jnp.ones((SUBLANES, BLK), F32)], axis=0)
        sm = jnp.where(causal, s_own[h], NEG)
        m0 = jnp.max(sm, axis=0, keepdims=True)
        p = jnp.exp(sm - m0)
        s_ref[0, h] = s_first[h]
        p_ref[0, h] = p.astype(BF16)
        start += [m0, jnp.sum(p, axis=0, keepdims=True), jnp.zeros((dh, BLK), F32)]

    def prev_of(t):
        return jnp.where(t == 0, i, t - 1), jnp.where(t == 0, nb, t - 1)

    def stage(t, src, dst, carry):
        blk_prev, row_prev = prev_of(t)
        nxt = jnp.minimum(t + 1, nb - 1)
        for h in range(2):
            s_ref[dst, h] = scores(nxt, h)
        pv = [values(blk_prev, h, p_ref[src, h]) for h in range(2)]
        out = []
        for h in range(2):
            m_i, l_i, acc = carry[3 * h:3 * h + 3]
            on = sel_ref[h, pl.ds(t, 1), :] > 0.5
            on_prev = sel_ref[h, pl.ds(row_prev, 1), :] > 0.5
            m_new = jnp.where(on, jnp.maximum(m_i, jnp.max(s_ref[src, h], axis=0, keepdims=True)), m_i)
            p = jnp.exp(s_ref[src, h] - m_new)
            p_ref[dst, h] = p.astype(BF16)
            alpha = jnp.exp(m_i - m_new)
            l_new = alpha * l_i + jnp.where(on, jnp.sum(p, axis=0, keepdims=True), 0.0)
            out += [m_new, l_new, alpha * (acc + jnp.where(on_prev, pv[h], 0.0))]
        return tuple(out)

    n_pairs = (i + 1) // 2
    fin = lax.fori_loop(0, n_pairs, lambda u, c: stage(2 * u + 1, 1, 0, stage(2 * u, 0, 1, c)), tuple(start))
    blk_prev, row_prev = prev_of(2 * n_pairs)
    outs = []
    for h in range(2):
        on_prev = sel_ref[h, pl.ds(row_prev, 1), :] > 0.5
        pv = values(blk_prev, h, p_ref[0, h])
        outs.append((fin[3 * h + 2] + jnp.where(on_prev, pv, 0.0)) / fin[3 * h + 1])
    o_ref[0] = _transpose_bf16(jnp.concatenate(outs, axis=0).astype(BF16)).astype(o_ref.dtype)


def _moba_attention(q, k, v, seq):
    B, _, width = q.shape
    BLK = MOBA_BLOCK
    nb = seq // BLK
    top = min(MOBA_TOPK, max(nb - 1, 1))
    return pl.pallas_call(
        functools.partial(_moba_kernel, seq=seq, top=top),
        grid=(B, width // LANES, nb),
        in_specs=[pl.BlockSpec((1, BLK, LANES), lambda b, h, i: (b, i, h)),
                  pl.BlockSpec((1, seq, LANES), lambda b, h, i: (b, 0, h)),
                  pl.BlockSpec((1, seq, LANES), lambda b, h, i: (b, 0, h))],
        out_specs=pl.BlockSpec((1, BLK, LANES), lambda b, h, i: (b, i, h)),
        out_shape=jax.ShapeDtypeStruct((B, seq, width), BF16),
        scratch_shapes=[pltpu.VMEM((nb, LANES), F32), pltpu.VMEM((LANES, seq), BF16),
                        pltpu.VMEM((2, nb + SUBLANES, BLK), F32),
                        pltpu.VMEM((2, 2, BLK, BLK), F32), pltpu.VMEM((2, 2, BLK, BLK), BF16)],
        compiler_params=_cparams("parallel", "parallel", "arbitrary"), name="moba_attn",
    )(q, k, v)


def _route(logits):
    n_fine = MOE_GROUPS * MOE_EXPERTS_PER_GROUP
    lane = lax.broadcasted_iota(I32, logits.shape, 1)
    lf = lane.astype(F32)
    is_coarse = (lane >= n_fine) & (lane < n_fine + MOE_GROUPS)
    pc = _masked_softmax(logits, is_coarse)
    gv = jnp.max(pc, axis=-1, keepdims=True)
    gidx = jnp.min(jnp.where(is_coarse & (pc == gv), lf - float(n_fine), float(MOE_GROUPS)), axis=-1, keepdims=True)
    in_group = (lane < n_fine) & ((lane // MOE_EXPERTS_PER_GROUP).astype(F32) == gidx)
    pf = _masked_softmax(logits, in_group)
    cand = jnp.where(in_group, pf, -1.0)
    m1 = jnp.max(cand, axis=-1, keepdims=True)
    i1 = jnp.min(jnp.where(cand == m1, lf, float(LANES)), axis=-1, keepdims=True)
    cand = jnp.where(lf == i1, -1.0, cand)
    m2 = jnp.max(cand, axis=-1, keepdims=True)
    i2 = jnp.min(jnp.where(cand == m2, lf, float(LANES)), axis=-1, keepdims=True)
    tot = m1 + m2
    return jnp.where(lf == i1, gv * (m1 / tot), jnp.where(lf == i2, gv * (m2 / tot), 0.0))


def _moe_kernel(x_ref, wr_ref, br_ref, wg_ref, wu_ref, wd_ref, lg_ref, lb_ref, o_ref, gates_ref, acc_ref, *, alpha):
    E, FF = MOE_EXPERTS_PER_GROUP, MOE_FF
    g = pl.program_id(1)
    x = x_ref[...]

    @pl.when(g == 0)
    def _():
        gates_ref[...] = _route(_dot3(x, wr_ref[...]) + br_ref[...])
        acc_ref[...] = jnp.zeros_like(acc_ref)

    xb = x.astype(BF16)
    hg = _dot(xb, wg_ref[0])
    hu = _dot(xb, wu_ref[0])
    h = hg * _sigmoid(hg) * hu
    gates = gates_ref[...]
    lane = lax.broadcasted_iota(I32, gates.shape, 1)
    parts = []
    for e in range(E):
        ge = jnp.sum(jnp.where(lane == g * E + e, gates, 0.0), axis=-1, keepdims=True)
        parts.append((h[:, e * FF:(e + 1) * FF] * ge).astype(BF16))
    acc_ref[...] += _dot(jnp.concatenate(parts, axis=-1), wd_ref[0])

    @pl.when(g == pl.num_programs(1) - 1)
    def _():
        o_ref[...] = _layer_norm(alpha * x + acc_ref[...], lg_ref[...], lb_ref[...])


def _moe(x2d, wr, br, wg, wu, wd, ln_g, ln_b, alpha):
    T, D = x2d.shape
    tm = min(ROW_TILE, T)
    NG = wg.shape[0]
    whole = lambda a: pl.BlockSpec(a.shape, lambda i, g: (0, 0))
    per_g = lambda a: pl.BlockSpec((1,) + a.shape[1:], lambda i, g: (g, 0, 0))
    return pl.pallas_call(
        functools.partial(_moe_kernel, alpha=alpha),
        grid=(T // tm, NG),
        in_specs=[pl.BlockSpec((tm, D), lambda i, g: (i, 0)), whole(wr), whole(br), per_g(wg), per_g(wu), per_g(wd),
                  whole(ln_g), whole(ln_b)],
        out_specs=pl.BlockSpec((tm, D), lambda i, g: (i, 0)),
        out_shape=jax.ShapeDtypeStruct((T, D), F32),
        scratch_shapes=[pltpu.VMEM((tm, LANES), F32), pltpu.VMEM((tm, D), F32)],
        compiler_params=_cparams("parallel", "arbitrary"), name="moe",
    )(x2d, wr, br, wg, wu, wd, ln_g, ln_b)


def _rope_tables(seq):
    inv = 1.0 / (ROPE_THETA ** (jnp.arange(0, HEAD_DIM, 2, dtype=F32) / HEAD_DIM))
    ang = jnp.arange(seq, dtype=F32)[:, None] * inv[None, :]
    cos, sin = jnp.cos(ang), jnp.sin(ang)
    reps = LANES // HEAD_DIM
    return jnp.tile(jnp.concatenate([cos, cos], -1), (1, reps)), jnp.tile(jnp.concatenate([-sin, sin], -1), (1, reps))


def _selection_constants(seq):
    ncp = seq // CMP_STRIDE
    nbs = seq // SEL_BLOCK
    starts = jnp.arange(ncp) * CMP_STRIDE
    bstart = jnp.arange(nbs) * SEL_BLOCK
    ovl = jnp.clip(jnp.minimum(starts[:, None] + CMP_BLOCK, bstart[None, :] + SEL_BLOCK)
                   - jnp.maximum(starts[:, None], bstart[None, :]), 0, CMP_BLOCK)
    return (ovl.astype(F32) / CMP_BLOCK).astype(BF16).T


def _moe_weights(w_coarse, b_coarse, w_fine, b_fine, w_gate, w_up, w_down):
    D = w_coarse.shape[0]
    NG, E, _, FF = w_gate.shape
    n_fine = NG * E
    wr = jnp.zeros((D, LANES), F32)
    wr = wr.at[:, :n_fine].set(w_fine.transpose(1, 0, 2).reshape(D, n_fine)).at[:, n_fine:n_fine + NG].set(w_coarse)
    br = jnp.zeros((1, LANES), F32)
    br = br.at[0, :n_fine].set(b_fine.reshape(n_fine)).at[0, n_fine:n_fine + NG].set(b_coarse)
    wg = w_gate.transpose(0, 2, 1, 3).reshape(NG, D, E * FF).astype(BF16)
    wu = w_up.transpose(0, 2, 1, 3).reshape(NG, D, E * FF).astype(BF16)
    wd = w_down.reshape(NG, E * FF, D).astype(BF16)
    return wr, br, wg, wu, wd


def _even_mixer(x2d, B, S, cos2, sin2, w_in, pe_k, pe_v, k_w1, k_w2, v_w1, v_w2,
                lam_re, lam_im, log_step, b_re, b_im, c_re, c_im, d_skip, glu_w, glu_b, w_out, ln_g, ln_b, alpha):
    G, Z, dh = NSA_KV_GROUPS, NSA_HEADS_PER_GROUP, HEAD_DIM
    kvw = NSA_KV_WIDTH
    D = x2d.shape[1]
    cuts = [NSA_WIDTH + i * kvw for i in range(7)]
    w_q, w_kc, w_vc, w_ks, w_vs, w_kw, w_vw = [w_in[:, a:b] for a, b in zip([0] + cuts[:-1], cuts)]
    n_gate = 3 * NSA_HEADS
    w_g = jnp.pad(w_in[:, cuts[-1]:cuts[-1] + n_gate], ((0, 0), (0, LANES - n_gate)))
    w_u = w_in[:, cuts[-1] + n_gate:]
    weights = [w_q.astype(BF16), jnp.concatenate([w_kc, w_ks, w_kw], 1).astype(BF16),
               jnp.concatenate([w_vc, w_vs, w_vw], 1).astype(BF16), w_u.astype(BF16), w_g.astype(BF16)]
    q, kk, vv, u, gates = _project(x2d, cos2, sin2, weights, ["rope", "rope", "plain", "plain", "sigmoid"],
                                   [BF16, BF16, BF16, F32, F32], S)

    def heads(t):
        return t.reshape(B, S, G, dh).transpose(0, 2, 1, 3)

    kc, ks, kw = [heads(kk[:, i * kvw:(i + 1) * kvw]) for i in range(3)]
    vc, vs, vw = [heads(vv[:, i * kvw:(i + 1) * kvw]) for i in range(3)]
    gates_t = gates[:, :n_gate].reshape(B, S, G, 3 * Z).transpose(0, 2, 3, 1)

    n_rows = S // CMP_STRIDE
    r = jnp.stack([kc, vc]).reshape(2, B * G, n_rows, CMP_STRIDE * dh)
    pe = jnp.stack([pe_k, pe_v]).reshape(2, 1, CMP_BLOCK * dh)
    w1 = jnp.stack([k_w1, v_w1]).astype(BF16)
    w2 = jnp.stack([k_w2, v_w2]).astype(BF16)
    cmp, cmp_t = _nsa_compress(r, pe, w1, w2, w2.transpose(0, 2, 1))
    kcm = cmp[0].reshape(B, G, n_rows, dh)
    vct = cmp_t[1].reshape(B, G, dh, n_rows)

    o_nsa = _nsa_attention(q.reshape(B, S, NSA_WIDTH), kcm, vct, ks, vs, kw, vw, gates_t, _selection_constants(S), S)

    C, L = S5_GROUP_CH, S5_CHUNK
    SG = u.shape[1] // C
    mt, pm, qt, al = _s5_operators(lam_re[:, None, :], lam_im[:, None, :], log_step[:, None, None],
                                   b_re.transpose(0, 2, 1), b_im.transpose(0, 2, 1), c_re, c_im)
    u_g = u.reshape(B, S // L, L, SG, C).transpose(3, 1, 0, 2, 4).reshape(SG, (S // L) * B, L * C)
    d_t = jnp.tile(d_skip.reshape(SG, 1, C), (1, 1, L))
    y = _s5_scan(u_g, mt, pm, qt, al, d_t, B)
    y = y.reshape(SG, S // L, B, L, C).transpose(2, 1, 3, 0, 4).reshape(B * S, SG * C)

    return _even_out(x2d, o_nsa.reshape(B * S, NSA_WIDTH), y, glu_w.astype(BF16), glu_b[None, :],
                     w_out[:NSA_WIDTH].astype(BF16), w_out[NSA_WIDTH:].astype(BF16), ln_g[None, :], ln_b[None, :], alpha)


def _odd_mixer(x2d, B, S, cos2, sin2, w_in, w_out, ln_g, ln_b, alpha):
    D = x2d.shape[1]
    q, k, v = _project(x2d, cos2, sin2, [w_in[:, i * D:(i + 1) * D].astype(BF16) for i in range(3)],
                       ["rope", "rope", "plain"], [BF16, BF16, BF16], S)
    attn = _moba_attention(q.reshape(B, S, D), k.reshape(B, S, D), v.reshape(B, S, D), S)
    return _odd_out(x2d, attn.reshape(B * S, D), w_out.astype(BF16), ln_g[None, :], ln_b[None, :], alpha)


def kernel(x, ev_w_in, nsa_pe_k, nsa_pe_v, nsa_cmp_k_w1, nsa_cmp_k_w2, nsa_cmp_v_w1, nsa_cmp_v_w2, s5_lambda_re, s5_lambda_im, s5_log_step, s5_b_re, s5_b_im, s5_c_re, s5_c_im, s5_d, s5_glu_w, s5_glu_b, ev_w_out, od_w_in, od_w_out, ln_mix_g, ln_mix_b, ln_ffn_g, ln_ffn_b, moe_w_coarse, moe_b_coarse, moe_w_fine, moe_b_fine, moe_w_gate, moe_w_up, moe_w_down):
    B, S, D = x.shape
    depth = ln_mix_g.shape[0]
    alpha = (2.0 * depth) ** 0.25
    cos2, sin2 = _rope_tables(S)
    h = x.reshape(B * S, D)
    for layer in range(depth):
        if layer % 2 == 0:
            e = layer // 2
            h = _even_mixer(h, B, S, cos2, sin2, ev_w_in[e], nsa_pe_k[e], nsa_pe_v[e], nsa_cmp_k_w1[e], nsa_cmp_k_w2[e],
                            nsa_cmp_v_w1[e], nsa_cmp_v_w2[e], s5_lambda_re[e], s5_lambda_im[e], s5_log_step[e],
                            s5_b_re[e], s5_b_im[e], s5_c_re[e], s5_c_im[e], s5_d[e], s5_glu_w[e], s5_glu_b[e],
                            ev_w_out[e], ln_mix_g[layer], ln_mix_b[layer], alpha)
        else:
            o = layer // 2
            h = _odd_mixer(h, B, S, cos2, sin2, od_w_in[o], od_w_out[o], ln_mix_g[layer], ln_mix_b[layer], alpha)
        wr, br, wg, wu, wd = _moe_weights(moe_w_coarse[layer], moe_b_coarse[layer], moe_w_fine[layer], moe_b_fine[layer],
                                          moe_w_gate[layer], moe_w_up[layer], moe_w_down[layer])
        h = _moe(h, wr, br, wg, wu, wd, ln_ffn_g[layer][None, :], ln_ffn_b[layer][None, :], alpha)
    return h.reshape(B, S, D)
```

```python
import functools
import math

import jax
import jax.numpy as jnp
from jax import lax
from jax.experimental import pallas as pl
from jax.experimental.pallas import tpu as pltpu

F32 = jnp.float32
BF16 = jnp.bfloat16
I32 = jnp.int32

HEAD_DIM = 64
ROPE_THETA = 10000.0
LN_EPS = 1e-5

NSA_KV_GROUPS = 2
NSA_HEADS_PER_GROUP = 4
NSA_HEADS = NSA_KV_GROUPS * NSA_HEADS_PER_GROUP
NSA_WIDTH = NSA_HEADS * HEAD_DIM
NSA_KV_WIDTH = NSA_KV_GROUPS * HEAD_DIM
CMP_BLOCK = 32
CMP_STRIDE = 16
SEL_BLOCK = 64
SEL_TOPN = 8
WINDOW = 256
NSA_QCHUNK = 128
NSA_KV_TILE = 512

S5_GROUP_CH = 16
S5_STATE = 64
S5_CHUNK = 64

MOBA_BLOCK = 256
MOBA_TOPK = 3

MOE_GROUPS = 4
MOE_EXPERTS_PER_GROUP = 8
MOE_FF = 128

LANES = 128
SUBLANES = 8
MXU_COLS = 256
VT_ROWS = HEAD_DIM + 16
ROW_TILE = 512
VMEM_LIMIT = 56 * 1024 * 1024
NEG = -1e30


def _cparams(*sem):
    return pltpu.CompilerParams(dimension_semantics=sem, vmem_limit_bytes=VMEM_LIMIT)


def _dot(a, b):
    return jnp.dot(a, b, preferred_element_type=F32)


def _dot_nt(a, b):
    return lax.dot_general(a, b, (((1,), (1,)), ((), ())), preferred_element_type=F32)


def _split(a):
    hi = a.astype(BF16)
    lo = (a - hi.astype(F32)).astype(BF16)
    return hi, lo


def _dot3(a, b):
    ah, al = _split(a)
    bh, bl = _split(b)
    return _dot(ah, bh) + _dot(ah, bl) + _dot(al, bh)


def _dot3_nt(a, b):
    ah, al = _split(a)
    bh, bl = _split(b)
    return _dot_nt(ah, bh) + _dot_nt(ah, bl) + _dot_nt(al, bh)


def _sigmoid(x):
    return 1.0 / (1.0 + jnp.exp(-x))


def _gelu_tanh(x):
    return 0.5 * x * (1.0 + jnp.tanh(math.sqrt(2.0 / math.pi) * (x + 0.044715 * (x * x * x))))


def _layer_norm(r, g, b):
    mu = jnp.mean(r, axis=-1, keepdims=True)
    d = r - mu
    var = jnp.mean(d * d, axis=-1, keepdims=True)
    return d * lax.rsqrt(var + LN_EPS) * g + b


def _masked_softmax(s, ok):
    sm = jnp.where(ok, s, NEG)
    m = jnp.max(sm, axis=-1, keepdims=True)
    p = jnp.where(ok, jnp.exp(sm - m), 0.0)
    return p / jnp.maximum(jnp.sum(p, axis=-1, keepdims=True), 1e-30)


def _rope_128(a, cos2, sin2):
    lane = lax.broadcasted_iota(I32, a.shape, 1)
    first_half = (lane & (HEAD_DIM - 1)) < (HEAD_DIM // 2)
    partner = jnp.where(first_half, pltpu.roll(a, LANES - HEAD_DIM // 2, 1), pltpu.roll(a, HEAD_DIM // 2, 1))
    return a * cos2 + partner * sin2


def _proj_kernel(x_ref, cos_ref, sin_ref, *refs, kinds):
    n = len(kinds)
    w_refs, o_refs = refs[:n], refs[n:]
    xb = x_ref[...].astype(BF16)
    cos2 = cos_ref[...]
    sin2 = sin_ref[...]
    for kind, w_ref, o_ref in zip(kinds, w_refs, o_refs):
        width = w_ref.shape[1]
        for c0 in range(0, width, MXU_COLS):
            cw = min(MXU_COLS, width - c0)
            acc = _dot(xb, w_ref[:, c0:c0 + cw])
            for l0 in range(0, cw, LANES):
                part = acc[:, l0:l0 + LANES]
                if kind == "rope":
                    part = _rope_128(part, cos2, sin2)
                elif kind == "sigmoid":
                    part = _sigmoid(part)
                o_ref[:, c0 + l0:c0 + l0 + LANES] = part.astype(o_ref.dtype)


def _project(x2d, cos2, sin2, weights, kinds, dtypes, seq):
    T, D = x2d.shape
    tm = min(ROW_TILE, seq)
    pos_tiles = seq // tm
    in_specs = [pl.BlockSpec((tm, D), lambda i: (i, 0)),
                pl.BlockSpec((tm, LANES), lambda i: (i % pos_tiles, 0)),
                pl.BlockSpec((tm, LANES), lambda i: (i % pos_tiles, 0))]
    in_specs += [pl.BlockSpec(w.shape, lambda i: (0, 0)) for w in weights]
    out_specs = [pl.BlockSpec((tm, w.shape[1]), lambda i: (i, 0)) for w in weights]
    out_shape = [jax.ShapeDtypeStruct((T, w.shape[1]), dt) for w, dt in zip(weights, dtypes)]
    return pl.pallas_call(
        functools.partial(_proj_kernel, kinds=tuple(kinds)),
        grid=(T // tm,), in_specs=in_specs, out_specs=out_specs, out_shape=out_shape,
        compiler_params=_cparams("parallel"), name="proj",
    )(x2d, cos2, sin2, *weights)


def _compress_kernel(r_ref, pe_ref, w1_ref, w2_ref, w2t_ref, o_ref, ot_ref):
    half = CMP_STRIDE * HEAD_DIM
    r = r_ref[0, 0]
    n = r.shape[0]
    top = _dot(r, w1_ref[0, :half, :])
    bot = _dot(r, w1_ref[0, half:, :])
    bias = _dot(pe_ref[0].astype(BF16), w1_ref[0])
    h = top + pltpu.roll(bot, n - 1, 0) + bias
    hb = _gelu_tanh(h).astype(BF16)
    o_ref[0, 0] = _dot(hb, w2_ref[0]).astype(o_ref.dtype)
    ot_ref[0, 0] = _dot_nt(w2t_ref[0], hb).astype(ot_ref.dtype)


def _nsa_compress(r, pe, w1, w2, w2t):
    _, bg, n, width = r.shape
    hid = w1.shape[2]
    return pl.pallas_call(
        _compress_kernel,
        grid=(2, bg),
        in_specs=[pl.BlockSpec((1, 1, n, width), lambda a, i: (a, i, 0, 0)),
                  pl.BlockSpec((1, 1, 2 * width), lambda a, i: (a, 0, 0)),
                  pl.BlockSpec((1, 2 * width, hid), lambda a, i: (a, 0, 0)),
                  pl.BlockSpec((1, hid, HEAD_DIM), lambda a, i: (a, 0, 0)),
                  pl.BlockSpec((1, HEAD_DIM, hid), lambda a, i: (a, 0, 0))],
        out_specs=[pl.BlockSpec((1, 1, n, HEAD_DIM), lambda a, i: (a, i, 0, 0)),
                   pl.BlockSpec((1, 1, HEAD_DIM, n), lambda a, i: (a, i, 0, 0))],
        out_shape=[jax.ShapeDtypeStruct((2, bg, n, HEAD_DIM), BF16),
                   jax.ShapeDtypeStruct((2, bg, HEAD_DIM, n), BF16)],
        compiler_params=_cparams("parallel", "parallel"), name="nsa_compress",
    )(r, pe, w1, w2, w2t)


def _exp_cols(s, ok):
    sm = jnp.where(ok, s, NEG)
    m = jnp.max(sm, axis=0, keepdims=True)
    p = jnp.exp(sm - jnp.where(m > 0.5 * NEG, m, 0.0))
    return p, jnp.sum(p, axis=0, keepdims=True)


def _nsa_attn_kernel(q_ref, kc_ref, vct_ref, ks_ref, vs_ref, kw_ref, vw_ref, gt_ref, c2st_ref, o_ref,
                     vst_ref, vwt_ref, sel_ref, s_ref, p_ref, *, seq, n_sel):
    Z, dh, QC, SB = NSA_HEADS_PER_GROUP, HEAD_DIM, NSA_QCHUNK, SEL_BLOCK
    W = Z * QC
    KT = 2 * SB
    ncp = seq // CMP_STRIDE
    nbs = seq // SEL_BLOCK
    c = pl.program_id(2)
    t0 = c * QC

    @pl.when(c == 0)
    def _():
        step = 4 * QC
        for r in range(seq // step):
            rows = slice(r * step, (r + 1) * step)
            vst_ref[0:dh, rows] = _transpose_bf16(vs_ref[0, 0, rows, :]).astype(BF16)
            vwt_ref[0:dh, rows] = _transpose_bf16(vw_ref[0, 0, rows, :]).astype(BF16)
        ones_row = jnp.where(lax.broadcasted_iota(I32, (VT_ROWS - dh, seq), 0) == 0, 1.0, 0.0).astype(BF16)
        vst_ref[dh:VT_ROWS, :] = ones_row
        vwt_ref[dh:VT_ROWS, :] = ones_row

    q_t = _transpose_bf16(q_ref[0]).astype(BF16)
    qt = jnp.concatenate([q_t[z * dh:(z + 1) * dh] for z in range(Z)], axis=1)
    qt = qt * jnp.asarray(dh ** -0.5, BF16)
    qpos = t0 + (lax.broadcasted_iota(I32, (1, W), 1) & (QC - 1))
    d0 = pl.multiple_of(t0, QC)
    band = QC + WINDOW
    w0 = pl.multiple_of(jnp.maximum(t0 - WINDOW, 0), QC)

    def tile4(row):
        return jnp.concatenate([row] * Z, axis=1)

    s_c = _dot(kc_ref[0, 0], qt)
    s_w = _dot(kw_ref[0, 0, pl.ds(w0, band), :], qt)
    s_d = _dot(ks_ref[0, 0, pl.ds(d0, KT), :], qt)
    s_ref[0] = _dot(ks_ref[0, 0, pl.ds(0, KT), :], qt)

    cmp_end = lax.broadcasted_iota(I32, (ncp, 1), 0) * CMP_STRIDE + (CMP_BLOCK - 1)
    e_c, l_c = _exp_cols(s_c, cmp_end <= qpos)
    p_c = e_c * (1.0 / jnp.maximum(l_c, 1e-30))
    o_c = _dot(vct_ref[0, 0], p_c.astype(BF16))

    p_sum = p_c[:, 0:QC]
    for z in range(1, Z):
        p_sum = p_sum + p_c[:, z * QC:(z + 1) * QC]
    p_hi, p_lo = _split(p_sum)
    imp = _dot(c2st_ref[...], p_hi) + _dot(c2st_ref[...], p_lo)

    jf = lax.broadcasted_iota(I32, (nbs, QC), 0).astype(F32)
    qblk = (qpos[:, :QC] // SEL_BLOCK).astype(F32)
    future = jf > qblk
    forced = (jf == 0.0) | (jf == qblk) | (jf == qblk - 1.0)
    score = jnp.where(future, -1.0, jnp.where(forced, 1e3, imp))
    selm = jnp.zeros((nbs, QC), F32)
    for _ in range(n_sel):
        m = jnp.max(score, axis=0, keepdims=True)
        idx = jnp.min(jnp.where(score == m, jf, float(nbs)), axis=0, keepdims=True)
        hit = jf == idx
        selm = jnp.where(hit, jnp.where(m >= 0.0, 1.0, 0.0), selm)
        score = jnp.where(hit, NEG, score)
    selw = tile4(selm)
    sel_ref[...] = jnp.where(lax.broadcasted_iota(I32, (nbs, W), 0) < 2 * c, selw, 0.0)

    brow = lax.broadcasted_iota(I32, (nbs, 1), 0)
    on_a = jnp.sum(jnp.where(brow == 2 * c, selw, 0.0), axis=0, keepdims=True)
    on_b = jnp.sum(jnp.where(brow == 2 * c + 1, selw, 0.0), axis=0, keepdims=True)
    krow = lax.broadcasted_iota(I32, (KT, 1), 0)
    ok_d = jnp.where(t0 + krow <= qpos, jnp.where(krow < SB, on_a, on_b), 0.0) > 0.5
    sm = jnp.where(ok_d, s_d, NEG)
    m0 = jnp.max(sm, axis=0, keepdims=True)
    p_d = jnp.exp(sm - m0)
    acc0 = _dot(vst_ref[:, pl.ds(d0, KT)], p_d.astype(BF16))
    p_ref[0] = jnp.zeros((KT, W), BF16)

    dist = qpos - (w0 + lax.broadcasted_iota(I32, (band, 1), 0))
    sm_w = jnp.where(jnp.where(dist >= 0, dist, WINDOW) < WINDOW, s_w, NEG)
    e_w = jnp.exp(sm_w - jnp.max(sm_w, axis=0, keepdims=True))
    acc_w = _dot(vwt_ref[:, pl.ds(w0, band)], e_w.astype(BF16))
    o_w = acc_w[0:dh] * (1.0 / jnp.maximum(acc_w[dh:dh + 1], 1e-30))

    def stage(t, src, dst, carry):
        m_i, acc = carry
        tp = jnp.maximum(t - 1, 0)
        k_next = pl.multiple_of(jnp.minimum(t + 1, c) * KT, KT)
        s_ref[dst] = _dot(ks_ref[0, 0, pl.ds(k_next, KT), :], qt)
        vt = vst_ref[:, pl.ds(pl.multiple_of(tp * KT, KT), KT)]
        pv_a = _dot(vt[:, :SB], p_ref[src, 0:SB, :])
        pv_b = _dot(vt[:, SB:], p_ref[src, SB:KT, :])
        on_a = sel_ref[pl.ds(2 * t, 1), :] > 0.5
        on_b = sel_ref[pl.ds(2 * t + 1, 1), :] > 0.5
        prev_a = sel_ref[pl.ds(2 * tp, 1), :] > 0.5
        prev_b = sel_ref[pl.ds(2 * tp + 1, 1), :] > 0.5
        mx_a = jnp.max(s_ref[src, 0:SB, :], axis=0, keepdims=True)
        mx_b = jnp.max(s_ref[src, SB:KT, :], axis=0, keepdims=True)
        m_new = jnp.maximum(m_i, jnp.maximum(jnp.where(on_a, mx_a, NEG), jnp.where(on_b, mx_b, NEG)))
        p_ref[dst] = jnp.exp(s_ref[src] - m_new).astype(BF16)
        acc_new = jnp.exp(m_i - m_new) * (acc + jnp.where(prev_a, pv_a, 0.0) + jnp.where(prev_b, pv_b, 0.0))
        return m_new, acc_new

    n_pairs = (c + 1) // 2
    _, acc_s = lax.fori_loop(0, n_pairs, lambda u, cr: stage(2 * u + 1, 1, 0, stage(2 * u, 0, 1, cr)), (m0, acc0))
    tl = jnp.maximum(2 * n_pairs - 1, 0)
    vt = vst_ref[:, pl.ds(pl.multiple_of(tl * KT, KT), KT)]
    last_a = sel_ref[pl.ds(2 * tl, 1), :] > 0.5
    last_b = sel_ref[pl.ds(2 * tl + 1, 1), :] > 0.5
    acc_s = (acc_s + jnp.where(last_a, _dot(vt[:, :SB], p_ref[0, 0:SB, :]), 0.0)
             + jnp.where(last_b, _dot(vt[:, SB:], p_ref[0, SB:KT, :]), 0.0))
    o_s = acc_s[0:dh] / jnp.maximum(acc_s[dh:dh + 1], 1e-30)

    gt = gt_ref[0, 0]
    g_c = jnp.concatenate([gt[3 * z:3 * z + 1] for z in range(Z)], axis=1)
    g_s = jnp.concatenate([gt[3 * z + 1:3 * z + 2] for z in range(Z)], axis=1)
    g_w = jnp.concatenate([gt[3 * z + 2:3 * z + 3] for z in range(Z)], axis=1)
    o_t = (g_c * o_c + g_s * o_s + g_w * o_w).astype(BF16)
    o_ref[0] = jnp.concatenate([_transpose_bf16(o_t[:, z * QC:(z + 1) * QC]) for z in range(Z)],
                               axis=1).astype(o_ref.dtype)


def _nsa_attention(q, kc, vct, ks, vs, kw, vw, gates_t, c2st, seq):
    B, G, _, dh = ks.shape
    Z, QC = NSA_HEADS_PER_GROUP, NSA_QCHUNK
    W = Z * QC
    KT = 2 * SEL_BLOCK
    ncp = seq // CMP_STRIDE
    nbs = seq // SEL_BLOCK
    full = lambda shp: pl.BlockSpec((1, 1) + shp, lambda b, g, c: (b, g, 0, 0))
    return pl.pallas_call(
        functools.partial(_nsa_attn_kernel, seq=seq, n_sel=min(SEL_TOPN, nbs)),
        grid=(B, G, seq // QC),
        in_specs=[pl.BlockSpec((1, QC, Z * dh), lambda b, g, c: (b, c, g)),
                  full((ncp, dh)), full((dh, ncp)), full((seq, dh)), full((seq, dh)), full((seq, dh)), full((seq, dh)),
                  pl.BlockSpec((1, 1, gates_t.shape[2], QC), lambda b, g, c: (b, g, 0, c)),
                  pl.BlockSpec((nbs, ncp), lambda b, g, c: (0, 0))],
        out_specs=pl.BlockSpec((1, QC, Z * dh), lambda b, g, c: (b, c, g)),
        out_shape=jax.ShapeDtypeStruct((B, seq, G * Z * dh), BF16),
        scratch_shapes=[pltpu.VMEM((VT_ROWS, seq), BF16), pltpu.VMEM((VT_ROWS, seq), BF16),
                        pltpu.VMEM((nbs, W), F32), pltpu.VMEM((2, KT, W), F32), pltpu.VMEM((2, KT, W), BF16)],
        compiler_params=_cparams("parallel", "parallel", "arbitrary"), name="nsa_attn",
    )(q, kc, vct, ks, vs, kw, vw, gates_t, c2st)


def _cmul(ar, ai, br, bi):
    return ar * br - ai * bi, ar * bi + ai * br


def _s5_ops_kernel(lre_ref, lim_ref, lstep_ref, bret_ref, bimt_ref, cre_ref, cim_ref,
                   mt_ref, pm_ref, qt_ref, al_ref):
    L, C, P = S5_CHUNK, S5_GROUP_CH, S5_STATE
    lr = lre_ref[0]
    li = lim_ref[0]
    step = jnp.exp(lstep_ref[0])
    mag = jnp.exp(lr * step)
    ar = mag * jnp.cos(li * step)
    ai = mag * jnp.sin(li * step)
    den = lr * lr + li * li
    fr = ((ar - 1.0) * lr + ai * li) / den
    fi = (ai * lr - (ar - 1.0) * li) / den
    bbr, bbi = _cmul(fr, fi, bret_ref[0], bimt_ref[0])

    up_r, up_i = jnp.ones((1, 1, P), F32), jnp.zeros((1, 1, P), F32)
    dn_r, dn_i = up_r, up_i
    sr, si = ar.reshape(1, 1, P), ai.reshape(1, 1, P)
    a1r, a1i = sr, si
    n = 1
    while n < L:
        tr, ti = _cmul(up_r, up_i, sr, si)
        up_r, up_i = jnp.concatenate([up_r, tr], 0), jnp.concatenate([up_i, ti], 0)
        tr, ti = _cmul(dn_r, dn_i, sr, si)
        dn_r, dn_i = jnp.concatenate([tr, dn_r], 0), jnp.concatenate([ti, dn_i], 0)
        sr, si = _cmul(sr, si, sr, si)
        n *= 2
    al_ref[0] = jnp.concatenate([sr.reshape(1, P), si.reshape(1, P)], -1)

    cr = cre_ref[0][None]
    ci = cim_ref[0][None]
    nr, ni = _cmul(up_r, up_i, a1r, a1i)
    wr, wi = _cmul(cr, ci, nr, ni)
    qt_ref[0] = jnp.concatenate([wr.reshape(L * C, P), -wi.reshape(L * C, P)], -1)
    er, ei = _cmul(dn_r, dn_i, bbr[None], bbi[None])
    pm_ref[0] = jnp.concatenate([er.reshape(L * C, P), ei.reshape(L * C, P)], -1).astype(pm_ref.dtype)
    w0r, w0i = _cmul(cr, ci, up_r, up_i)
    kt = _dot3_nt(bbr, w0r.reshape(L * C, P)) - _dot3_nt(bbi, w0i.reshape(L * C, P))
    lane = lax.broadcasted_iota(I32, kt.shape, 1)
    mt_ref[0, 0:C, :] = kt.astype(mt_ref.dtype)
    for s in range(1, L):
        shifted = jnp.where(lane >= s * C, pltpu.roll(kt, s * C, 1), 0.0)
        mt_ref[0, s * C:(s + 1) * C, :] = shifted.astype(mt_ref.dtype)


def _s5_operators(lre, lim, lstep, bret, bimt, cre, cim):
    G = lre.shape[0]
    L, C, P = S5_CHUNK, S5_GROUP_CH, S5_STATE
    vec = pl.BlockSpec((1, 1, P), lambda g: (g, 0, 0))
    mat = pl.BlockSpec((1, C, P), lambda g: (g, 0, 0))
    return pl.pallas_call(
        _s5_ops_kernel,
        grid=(G,),
        in_specs=[vec, vec, pl.BlockSpec((1, 1, 1), lambda g: (g, 0, 0)), mat, mat, mat, mat],
        out_specs=[pl.BlockSpec((1, L * C, L * C), lambda g: (g, 0, 0)),
                   pl.BlockSpec((1, L * C, 2 * P), lambda g: (g, 0, 0)),
                   pl.BlockSpec((1, L * C, 2 * P), lambda g: (g, 0, 0)),
                   pl.BlockSpec((1, 1, 2 * P), lambda g: (g, 0, 0))],
        out_shape=[jax.ShapeDtypeStruct((G, L * C, L * C), BF16),
                   jax.ShapeDtypeStruct((G, L * C, 2 * P), BF16),
                   jax.ShapeDtypeStruct((G, L * C, 2 * P), F32),
                   jax.ShapeDtypeStruct((G, 1, 2 * P), F32)],
        compiler_params=_cparams("parallel"), name="s5_operators",
    )(lre, lim, lstep, bret, bimt, cre, cim)


def _s5_scan_kernel(u_ref, mt_ref, pm_ref, qt_ref, al_ref, d_ref, y_ref, xin_ref, *, batch):
    P = S5_STATE
    rows = u_ref.shape[1]
    u = u_ref[0]
    ub = u.astype(BF16)
    y_ref[0] = _dot(ub, mt_ref[0]) + d_ref[0] * u
    xloc = _dot(ub, pm_ref[0])
    al = al_ref[0]
    lane = lax.broadcasted_iota(I32, (1, 2 * P), 1)
    mul_same = jnp.where(lane < P, al, pltpu.roll(al, P, 1))
    mul_swap = jnp.where(lane < P, -pltpu.roll(al, P, 1), al)
    xin_ref[...] = xloc

    def carry_step(k, state):
        r0 = pl.multiple_of(k * batch, batch)
        loc = xin_ref[pl.ds(r0, batch), :]
        xin_ref[pl.ds(r0, batch), :] = state
        return state * mul_same + pltpu.roll(state, P, 1) * mul_swap + loc

    lax.fori_loop(0, rows // batch, carry_step, jnp.zeros((batch, 2 * P), F32))
    y_ref[0] += _dot3_nt(xin_ref[...], qt_ref[0])


def _s5_scan(u, mt, pm, qt, al, d, batch):
    G, rows, width = u.shape
    P2 = 2 * S5_STATE
    per_g = lambda shp: pl.BlockSpec((1,) + shp, lambda g: (g, 0, 0))
    return pl.pallas_call(
        functools.partial(_s5_scan_kernel, batch=batch),
        grid=(G,),
        in_specs=[per_g((rows, width)), per_g((width, width)), per_g((width, P2)), per_g((width, P2)),
                  per_g((1, P2)), per_g((1, width))],
        out_specs=per_g((rows, width)),
        out_shape=jax.ShapeDtypeStruct((G, rows, width), F32),
        scratch_shapes=[pltpu.VMEM((rows, P2), F32)],
        compiler_params=_cparams("parallel"), name="s5_scan",
    )(u, mt, pm, qt, al, d)


def _even_out_kernel(x_ref, nsa_ref, y_ref, gw_ref, gb_ref, wa_ref, wb_ref, lg_ref, lb_ref, o_ref, *, alpha):
    g = _gelu_tanh(y_ref[...])
    s5 = g * _sigmoid(_dot(g.astype(BF16), gw_ref[...]) + gb_ref[...])
    mix = _dot(nsa_ref[...], wa_ref[...]) + _dot(s5.astype(BF16), wb_ref[...])
    o_ref[...] = _layer_norm(alpha * x_ref[...] + mix, lg_ref[...], lb_ref[...])


def _even_out(x2d, o_nsa, y_s5, glu_w, glu_b, wa, wb, ln_g, ln_b, alpha):
    T, D = x2d.shape
    tm = min(ROW_TILE, T)
    row = lambda w: pl.BlockSpec((tm, w), lambda i: (i, 0))
    whole = lambda a: pl.BlockSpec(a.shape, lambda i: (0, 0))
    return pl.pallas_call(
        functools.partial(_even_out_kernel, alpha=alpha),
        grid=(T // tm,),
        in_specs=[row(D), row(o_nsa.shape[1]), row(y_s5.shape[1]), whole(glu_w), whole(glu_b), whole(wa), whole(wb),
                  whole(ln_g), whole(ln_b)],
        out_specs=row(D), out_shape=jax.ShapeDtypeStruct((T, D), F32),
        compiler_params=_cparams("parallel"), name="even_out",
    )(x2d, o_nsa, y_s5, glu_w, glu_b, wa, wb, ln_g, ln_b)


def _odd_out_kernel(x_ref, at_ref, w_ref, lg_ref, lb_ref, o_ref, *, alpha):
    attn = _transpose_bf16(at_ref[0]).astype(BF16)
    mix = _dot(attn, w_ref[...])
    o_ref[...] = _layer_norm(alpha * x_ref[...] + mix, lg_ref[...], lb_ref[...])


def _odd_out(x2d, attn_t, w, ln_g, ln_b, alpha):
    T, D = x2d.shape
    _, width, seq = attn_t.shape
    tm = min(ROW_TILE, seq)
    tiles = seq // tm
    row = lambda wd: pl.BlockSpec((tm, wd), lambda i: (i, 0))
    whole = lambda a: pl.BlockSpec(a.shape, lambda i: (0, 0))
    return pl.pallas_call(
        functools.partial(_odd_out_kernel, alpha=alpha),
        grid=(T // tm,),
        in_specs=[row(D), pl.BlockSpec((1, width, tm), lambda i: (i // tiles, 0, i % tiles)),
                  whole(w), whole(ln_g), whole(ln_b)],
        out_specs=row(D), out_shape=jax.ShapeDtypeStruct((T, D), F32),
        compiler_params=_cparams("parallel"), name="odd_out",
    )(x2d, attn_t, w, ln_g, ln_b)


def _eye(n):
    return (lax.broadcasted_iota(I32, (n, n), 0) == lax.broadcasted_iota(I32, (n, n), 1)).astype(BF16)


def _transpose_bf16(a):
    return _dot_nt(_eye(a.shape[1]), a)


def _moba_kernel(q_ref, k_ref, v_ref, o_ref, kmean_ref, vt_ref, sel_ref, s_ref, p_ref, *, seq, top):
    BLK, dh = MOBA_BLOCK, HEAD_DIM
    nb = seq // BLK
    i = pl.program_id(2)

    @pl.when(i == 0)
    def _():
        kmean_ref[...] = jnp.mean(k_ref[0].astype(F32).reshape(nb, BLK, 2 * dh), axis=1)
        for r in range(nb):
            cols = slice(r * BLK, (r + 1) * BLK)
            v_t = _transpose_bf16(v_ref[0, cols, :]).astype(BF16)
            for h in range(2):
                vt_ref[h, 0:dh, cols] = v_t[h * dh:(h + 1) * dh]
        ones_row = jnp.where(lax.broadcasted_iota(I32, (VT_ROWS - dh, seq), 0) == 0, 1.0, 0.0).astype(BF16)
        for h in range(2):
            vt_ref[h, dh:VT_ROWS, :] = ones_row

    jf = lax.broadcasted_iota(I32, (nb, BLK), 0).astype(F32)
    i_f = i.astype(F32)
    causal = lax.broadcasted_iota(I32, (BLK, BLK), 0) <= lax.broadcasted_iota(I32, (BLK, BLK), 1)

    heads = (slice(0, dh), slice(dh, 2 * dh))
    qs = [q_ref[0, :, hs] * jnp.asarray(dh ** -0.5, BF16) for hs in heads]

    def scores(j, h):
        c0 = pl.multiple_of(j * BLK, BLK)
        return _dot_nt(k_ref[0, pl.ds(c0, BLK), heads[h]], qs[h])

    def values(j, h, p):
        c0 = pl.multiple_of(j * BLK, BLK)
        return _dot(vt_ref[h, :, pl.ds(c0, BLK)], p)

    gates = []
    for h in range(2):
        km_hi, km_lo = _split(kmean_ref[:, heads[h]])
        gates.append(_dot_nt(km_hi, qs[h]) + _dot_nt(km_lo, qs[h]))
    s_own = [scores(i, h) for h in range(2)]
    s_first = [scores(0, h) for h in range(2)]

    start = []
    for h in range(2):
        gate = jnp.where(jf < i_f, gates[h], NEG)
        selm = jnp.zeros((nb, BLK), F32)
        for _ in range(top):
            m = jnp.max(gate, axis=0, keepdims=True)
            idx = jnp.min(jnp.where(gate == m, jf, float(nb)), axis=0, keepdims=True)
            hit = jf == idx
            selm = jnp.where(hit, jnp.where(m > 0.5 * NEG, 1.0, 0.0), selm)
            gate = jnp.where(hit, 2.0 * NEG, gate)
        sel_ref[h] = jnp.concatenate([selm, jnp.ones((SUBLANES, BLK), F32)], axis=0)
        sm = jnp.where(causal, s_own[h], NEG)
        m0 = jnp.max(sm, axis=0, keepdims=True)
        p = jnp.exp(sm - m0)
        s_ref[0, h] = s_first[h]
        p_ref[0, h] = p.astype(BF16)
        start += [m0, jnp.zeros((VT_ROWS, BLK), F32)]

    def prev_of(t):
        return jnp.where(t == 0, i, t - 1), jnp.where(t == 0, nb, t - 1)

    def stage(t, src, dst, carry):
        blk_prev, row_prev = prev_of(t)
        nxt = jnp.minimum(t + 1, nb - 1)
        for h in range(2):
            s_ref[dst, h] = scores(nxt, h)
        pv = [values(blk_prev, h, p_ref[src, h]) for h in range(2)]
        out = []
        for h in range(2):
            m_i, acc = carry[2 * h:2 * h + 2]
            on = sel_ref[h, pl.ds(t, 1), :] > 0.5
            on_prev = sel_ref[h, pl.ds(row_prev, 1), :] > 0.5
            m_new = jnp.where(on, jnp.maximum(m_i, jnp.max(s_ref[src, h], axis=0, keepdims=True)), m_i)
            p_ref[dst, h] = jnp.exp(s_ref[src, h] - m_new).astype(BF16)
            out += [m_new, jnp.exp(m_i - m_new) * (acc + jnp.where(on_prev, pv[h], 0.0))]
        return tuple(out)

    n_pairs = (i + 1) // 2
    fin = lax.fori_loop(0, n_pairs, lambda u, c: stage(2 * u + 1, 1, 0, stage(2 * u, 0, 1, c)), tuple(start))
    blk_prev, row_prev = prev_of(2 * n_pairs)
    outs = []
    for h in range(2):
        on_prev = sel_ref[h, pl.ds(row_prev, 1), :] > 0.5
        acc = fin[2 * h + 1] + jnp.where(on_prev, values(blk_prev, h, p_ref[0, h]), 0.0)
        outs.append(acc[0:dh] / acc[dh:dh + 1])
    o_ref[0] = jnp.concatenate(outs, axis=0).astype(o_ref.dtype)


def _moba_attention(q, k, v, seq):
    B, _, width = q.shape
    BLK = MOBA_BLOCK
    nb = seq // BLK
    top = min(MOBA_TOPK, max(nb - 1, 1))
    return pl.pallas_call(
        functools.partial(_moba_kernel, seq=seq, top=top),
        grid=(B, width // LANES, nb),
        in_specs=[pl.BlockSpec((1, BLK, LANES), lambda b, h, i: (b, i, h)),
                  pl.BlockSpec((1, seq, LANES), lambda b, h, i: (b, 0, h)),
                  pl.BlockSpec((1, seq, LANES), lambda b, h, i: (b, 0, h))],
        out_specs=pl.BlockSpec((1, LANES, BLK), lambda b, h, i: (b, h, i)),
        out_shape=jax.ShapeDtypeStruct((B, width, seq), BF16),
        scratch_shapes=[pltpu.VMEM((nb, LANES), F32), pltpu.VMEM((2, VT_ROWS, seq), BF16),
                        pltpu.VMEM((2, nb + SUBLANES, BLK), F32),
                        pltpu.VMEM((2, 2, BLK, BLK), F32), pltpu.VMEM((2, 2, BLK, BLK), BF16)],
        compiler_params=_cparams("parallel", "parallel", "arbitrary"), name="moba_attn",
    )(q, k, v)


def _route(logits):
    n_fine = MOE_GROUPS * MOE_EXPERTS_PER_GROUP
    lane = lax.broadcasted_iota(I32, logits.shape, 1)
    lf = lane.astype(F32)
    is_coarse = (lane >= n_fine) & (lane < n_fine + MOE_GROUPS)
    pc = _masked_softmax(logits, is_coarse)
    gv = jnp.max(pc, axis=-1, keepdims=True)
    gidx = jnp.min(jnp.where(is_coarse & (pc == gv), lf - float(n_fine), float(MOE_GROUPS)), axis=-1, keepdims=True)
    in_group = (lane < n_fine) & ((lane // MOE_EXPERTS_PER_GROUP).astype(F32) == gidx)
    pf = _masked_softmax(logits, in_group)
    cand = jnp.where(in_group, pf, -1.0)
    m1 = jnp.max(cand, axis=-1, keepdims=True)
    i1 = jnp.min(jnp.where(cand == m1, lf, float(LANES)), axis=-1, keepdims=True)
    cand = jnp.where(lf == i1, -1.0, cand)
    m2 = jnp.max(cand, axis=-1, keepdims=True)
    i2 = jnp.min(jnp.where(cand == m2, lf, float(LANES)), axis=-1, keepdims=True)
    tot = m1 + m2
    return jnp.where(lf == i1, gv * (m1 / tot), jnp.where(lf == i2, gv * (m2 / tot), 0.0))


def _moe_kernel(x_ref, wr_ref, br_ref, wg_ref, wu_ref, wd_ref, lg_ref, lb_ref, o_ref, gates_ref, acc_ref, *, alpha):
    E, FF = MOE_EXPERTS_PER_GROUP, MOE_FF
    g = pl.program_id(1)
    x = x_ref[...]

    @pl.when(g == 0)
    def _():
        gates_ref[...] = _route(_dot3(x, wr_ref[...]) + br_ref[...])
        acc_ref[...] = jnp.zeros_like(acc_ref)

    xb = x.astype(BF16)
    hg = _dot(xb, wg_ref[0])
    hu = _dot(xb, wu_ref[0])
    h = hg * _sigmoid(hg) * hu
    gates = gates_ref[...]
    lane = lax.broadcasted_iota(I32, gates.shape, 1)
    parts = []
    for e in range(E):
        ge = jnp.sum(jnp.where(lane == g * E + e, gates, 0.0), axis=-1, keepdims=True)
        parts.append((h[:, e * FF:(e + 1) * FF] * ge).astype(BF16))
    acc_ref[...] += _dot(jnp.concatenate(parts, axis=-1), wd_ref[0])

    @pl.when(g == pl.num_programs(1) - 1)
    def _():
        o_ref[...] = _layer_norm(alpha * x + acc_ref[...], lg_ref[...], lb_ref[...])


def _moe(x2d, wr, br, wg, wu, wd, ln_g, ln_b, alpha):
    T, D = x2d.shape
    tm = min(ROW_TILE, T)
    NG = wg.shape[0]
    whole = lambda a: pl.BlockSpec(a.shape, lambda i, g: (0, 0))
    per_g = lambda a: pl.BlockSpec((1,) + a.shape[1:], lambda i, g: (g, 0, 0))
    return pl.pallas_call(
        functools.partial(_moe_kernel, alpha=alpha),
        grid=(T // tm, NG),
        in_specs=[pl.BlockSpec((tm, D), lambda i, g: (i, 0)), whole(wr), whole(br), per_g(wg), per_g(wu), per_g(wd),
                  whole(ln_g), whole(ln_b)],
        out_specs=pl.BlockSpec((tm, D), lambda i, g: (i, 0)),
        out_shape=jax.ShapeDtypeStruct((T, D), F32),
        scratch_shapes=[pltpu.VMEM((tm, LANES), F32), pltpu.VMEM((tm, D), F32)],
        compiler_params=_cparams("parallel", "arbitrary"), name="moe",
    )(x2d, wr, br, wg, wu, wd, ln_g, ln_b)


def _rope_tables(seq):
    inv = 1.0 / (ROPE_THETA ** (jnp.arange(0, HEAD_DIM, 2, dtype=F32) / HEAD_DIM))
    ang = jnp.arange(seq, dtype=F32)[:, None] * inv[None, :]
    cos, sin = jnp.cos(ang), jnp.sin(ang)
    reps = LANES // HEAD_DIM
    return jnp.tile(jnp.concatenate([cos, cos], -1), (1, reps)), jnp.tile(jnp.concatenate([-sin, sin], -1), (1, reps))


def _selection_constants(seq):
    ncp = seq // CMP_STRIDE
    nbs = seq // SEL_BLOCK
    starts = jnp.arange(ncp) * CMP_STRIDE
    bstart = jnp.arange(nbs) * SEL_BLOCK
    ovl = jnp.clip(jnp.minimum(starts[:, None] + CMP_BLOCK, bstart[None, :] + SEL_BLOCK)
                   - jnp.maximum(starts[:, None], bstart[None, :]), 0, CMP_BLOCK)
    return (ovl.astype(F32) / CMP_BLOCK).astype(BF16).T


def _moe_weights(w_coarse, b_coarse, w_fine, b_fine, w_gate, w_up, w_down):
    D = w_coarse.shape[0]
    NG, E, _, FF = w_gate.shape
    n_fine = NG * E
    wr = jnp.zeros((D, LANES), F32)
    wr = wr.at[:, :n_fine].set(w_fine.transpose(1, 0, 2).reshape(D, n_fine)).at[:, n_fine:n_fine + NG].set(w_coarse)
    br = jnp.zeros((1, LANES), F32)
    br = br.at[0, :n_fine].set(b_fine.reshape(n_fine)).at[0, n_fine:n_fine + NG].set(b_coarse)
    wg = w_gate.transpose(0, 2, 1, 3).reshape(NG, D, E * FF).astype(BF16)
    wu = w_up.transpose(0, 2, 1, 3).reshape(NG, D, E * FF).astype(BF16)
    wd = w_down.reshape(NG, E * FF, D).astype(BF16)
    return wr, br, wg, wu, wd


def _even_mixer(x2d, B, S, cos2, sin2, w_in, pe_k, pe_v, k_w1, k_w2, v_w1, v_w2,
                lam_re, lam_im, log_step, b_re, b_im, c_re, c_im, d_skip, glu_w, glu_b, w_out, ln_g, ln_b, alpha):
    G, Z, dh = NSA_KV_GROUPS, NSA_HEADS_PER_GROUP, HEAD_DIM
    kvw = NSA_KV_WIDTH
    D = x2d.shape[1]
    cuts = [NSA_WIDTH + i * kvw for i in range(7)]
    w_q, w_kc, w_vc, w_ks, w_vs, w_kw, w_vw = [w_in[:, a:b] for a, b in zip([0] + cuts[:-1], cuts)]
    n_gate = 3 * NSA_HEADS
    w_g = jnp.pad(w_in[:, cuts[-1]:cuts[-1] + n_gate], ((0, 0), (0, LANES - n_gate)))
    w_u = w_in[:, cuts[-1] + n_gate:]
    weights = [w_q.astype(BF16), jnp.concatenate([w_kc, w_ks, w_kw], 1).astype(BF16),
               jnp.concatenate([w_vc, w_vs, w_vw], 1).astype(BF16), w_u.astype(BF16), w_g.astype(BF16)]
    q, kk, vv, u, gates = _project(x2d, cos2, sin2, weights, ["rope", "rope", "plain", "plain", "sigmoid"],
                                   [BF16, BF16, BF16, F32, F32], S)

    def heads(t):
        return t.reshape(B, S, G, dh).transpose(0, 2, 1, 3)

    kc, ks, kw = [heads(kk[:, i * kvw:(i + 1) * kvw]) for i in range(3)]
    vc, vs, vw = [heads(vv[:, i * kvw:(i + 1) * kvw]) for i in range(3)]
    gates_t = gates[:, :n_gate].reshape(B, S, G, 3 * Z).transpose(0, 2, 3, 1)

    n_rows = S // CMP_STRIDE
    r = jnp.stack([kc, vc]).reshape(2, B * G, n_rows, CMP_STRIDE * dh)
    pe = jnp.stack([pe_k, pe_v]).reshape(2, 1, CMP_BLOCK * dh)
    w1 = jnp.stack([k_w1, v_w1]).astype(BF16)
    w2 = jnp.stack([k_w2, v_w2]).astype(BF16)
    cmp, cmp_t = _nsa_compress(r, pe, w1, w2, w2.transpose(0, 2, 1))
    kcm = cmp[0].reshape(B, G, n_rows, dh)
    vct = cmp_t[1].reshape(B, G, dh, n_rows)

    o_nsa = _nsa_attention(q.reshape(B, S, NSA_WIDTH), kcm, vct, ks, vs, kw, vw, gates_t, _selection_constants(S), S)

    C, L = S5_GROUP_CH, S5_CHUNK
    SG = u.shape[1] // C
    mt, pm, qt, al = _s5_operators(lam_re[:, None, :], lam_im[:, None, :], log_step[:, None, None],
                                   b_re.transpose(0, 2, 1), b_im.transpose(0, 2, 1), c_re, c_im)
    u_g = u.reshape(B, S // L, L, SG, C).transpose(3, 1, 0, 2, 4).reshape(SG, (S // L) * B, L * C)
    d_t = jnp.tile(d_skip.reshape(SG, 1, C), (1, 1, L))
    y = _s5_scan(u_g, mt, pm, qt, al, d_t, B)
    y = y.reshape(SG, S // L, B, L, C).transpose(2, 1, 3, 0, 4).reshape(B * S, SG * C)

    return _even_out(x2d, o_nsa.reshape(B * S, NSA_WIDTH), y, glu_w.astype(BF16), glu_b[None, :],
                     w_out[:NSA_WIDTH].astype(BF16), w_out[NSA_WIDTH:].astype(BF16), ln_g[None, :], ln_b[None, :], alpha)


def _odd_mixer(x2d, B, S, cos2, sin2, w_in, w_out, ln_g, ln_b, alpha):
    D = x2d.shape[1]
    q, k, v = _project(x2d, cos2, sin2, [w_in[:, i * D:(i + 1) * D].astype(BF16) for i in range(3)],
                       ["rope", "rope", "plain"], [BF16, BF16, BF16], S)
    attn_t = _moba_attention(q.reshape(B, S, D), k.reshape(B, S, D), v.reshape(B, S, D), S)
    return _odd_out(x2d, attn_t, w_out.astype(BF16), ln_g[None, :], ln_b[None, :], alpha)


def kernel(x, ev_w_in, nsa_pe_k, nsa_pe_v, nsa_cmp_k_w1, nsa_cmp_k_w2, nsa_cmp_v_w1, nsa_cmp_v_w2, s5_lambda_re, s5_lambda_im, s5_log_step, s5_b_re, s5_b_im, s5_c_re, s5_c_im, s5_d, s5_glu_w, s5_glu_b, ev_w_out, od_w_in, od_w_out, ln_mix_g, ln_mix_b, ln_ffn_g, ln_ffn_b, moe_w_coarse, moe_b_coarse, moe_w_fine, moe_b_fine, moe_w_gate, moe_w_up, moe_w_down):
    B, S, D = x.shape
    depth = ln_mix_g.shape[0]
    alpha = (2.0 * depth) ** 0.25
    cos2, sin2 = _rope_tables(S)
    h = x.reshape(B * S, D)
    for layer in range(depth):
        if layer % 2 == 0:
            e = layer // 2
            h = _even_mixer(h, B, S, cos2, sin2, ev_w_in[e], nsa_pe_k[e], nsa_pe_v[e], nsa_cmp_k_w1[e], nsa_cmp_k_w2[e],
                            nsa_cmp_v_w1[e], nsa_cmp_v_w2[e], s5_lambda_re[e], s5_lambda_im[e], s5_log_step[e],
                            s5_b_re[e], s5_b_im[e], s5_c_re[e], s5_c_im[e], s5_d[e], s5_glu_w[e], s5_glu_b[e],
                            ev_w_out[e], ln_mix_g[layer], ln_mix_b[layer], alpha)
        else:
            o = layer // 2
            h = _odd_mixer(h, B, S, cos2, sin2, od_w_in[o], od_w_out[o], ln_mix_g[layer], ln_mix_b[layer], alpha)
        wr, br, wg, wu, wd = _moe_weights(moe_w_coarse[layer], moe_b_coarse[layer], moe_w_fine[layer], moe_b_fine[layer],
                                          moe_w_gate[layer], moe_w_up[layer], moe_w_down[layer])
        h = _moe(h, wr, br, wg, wu, wd, ln_ffn_g[layer][None, :], ln_ffn_b[layer][None, :], alpha)
    return h.reshape(B, S, D)
```

```python
import functools
import math

import jax
import jax.numpy as jnp
from jax import lax
from jax.experimental import pallas as pl
from jax.experimental.pallas import tpu as pltpu

F32 = jnp.float32
BF16 = jnp.bfloat16
I32 = jnp.int32

HEAD_DIM = 64
ROPE_THETA = 10000.0
LN_EPS = 1e-5

NSA_KV_GROUPS = 2
NSA_HEADS_PER_GROUP = 4
NSA_HEADS = NSA_KV_GROUPS * NSA_HEADS_PER_GROUP
NSA_WIDTH = NSA_HEADS * HEAD_DIM
NSA_KV_WIDTH = NSA_KV_GROUPS * HEAD_DIM
CMP_BLOCK = 32
CMP_STRIDE = 16
SEL_BLOCK = 64
SEL_TOPN = 8
WINDOW = 256
NSA_QCHUNK = 128
NSA_KV_TILE = 512

S5_GROUP_CH = 16
S5_STATE = 64
S5_CHUNK = 64

MOBA_BLOCK = 256
MOBA_TOPK = 3
MOBA_HEADS_PER_STEP = 4

MOE_GROUPS = 4
MOE_EXPERTS_PER_GROUP = 8
MOE_FF = 128

LANES = 128
SUBLANES = 8
MXU_COLS = 256
VT_ROWS = HEAD_DIM + 16
ROW_TILE = 512
VMEM_LIMIT = 56 * 1024 * 1024
NEG = -1e30


def _cparams(*sem):
    return pltpu.CompilerParams(dimension_semantics=sem, vmem_limit_bytes=VMEM_LIMIT)


def _dot(a, b):
    return jnp.dot(a, b, preferred_element_type=F32)


def _dot_nt(a, b):
    return lax.dot_general(a, b, (((1,), (1,)), ((), ())), preferred_element_type=F32)


def _split(a):
    hi = a.astype(BF16)
    lo = (a - hi.astype(F32)).astype(BF16)
    return hi, lo


def _dot3(a, b):
    ah, al = _split(a)
    bh, bl = _split(b)
    return _dot(ah, bh) + _dot(ah, bl) + _dot(al, bh)


def _dot3_nt(a, b):
    ah, al = _split(a)
    bh, bl = _split(b)
    return _dot_nt(ah, bh) + _dot_nt(ah, bl) + _dot_nt(al, bh)


def _sigmoid(x):
    return 1.0 / (1.0 + jnp.exp(-x))


def _gelu_tanh(x):
    return 0.5 * x * (1.0 + jnp.tanh(math.sqrt(2.0 / math.pi) * (x + 0.044715 * (x * x * x))))


def _layer_norm(r, g, b):
    mu = jnp.mean(r, axis=-1, keepdims=True)
    d = r - mu
    var = jnp.mean(d * d, axis=-1, keepdims=True)
    return d * lax.rsqrt(var + LN_EPS) * g + b


def _masked_softmax(s, ok):
    sm = jnp.where(ok, s, NEG)
    m = jnp.max(sm, axis=-1, keepdims=True)
    p = jnp.where(ok, jnp.exp(sm - m), 0.0)
    return p / jnp.maximum(jnp.sum(p, axis=-1, keepdims=True), 1e-30)


def _rope_128(a, cos2, sin2):
    lane = lax.broadcasted_iota(I32, a.shape, 1)
    first_half = (lane & (HEAD_DIM - 1)) < (HEAD_DIM // 2)
    partner = jnp.where(first_half, pltpu.roll(a, LANES - HEAD_DIM // 2, 1), pltpu.roll(a, HEAD_DIM // 2, 1))
    return a * cos2 + partner * sin2


def _proj_kernel(x_ref, cos_ref, sin_ref, *refs, kinds):
    n = len(kinds)
    w_refs, o_refs = refs[:n], refs[n:]
    xb = x_ref[...].astype(BF16)
    cos2 = cos_ref[...]
    sin2 = sin_ref[...]
    for kind, w_ref, o_ref in zip(kinds, w_refs, o_refs):
        width = w_ref.shape[1]
        for c0 in range(0, width, MXU_COLS):
            cw = min(MXU_COLS, width - c0)
            acc = _dot(xb, w_ref[:, c0:c0 + cw])
            for l0 in range(0, cw, LANES):
                part = acc[:, l0:l0 + LANES]
                if kind == "rope":
                    part = _rope_128(part, cos2, sin2)
                elif kind == "sigmoid":
                    part = _sigmoid(part)
                o_ref[:, c0 + l0:c0 + l0 + LANES] = part.astype(o_ref.dtype)


def _project(x2d, cos2, sin2, weights, kinds, dtypes, seq):
    T, D = x2d.shape
    tm = min(ROW_TILE, seq)
    pos_tiles = seq // tm
    in_specs = [pl.BlockSpec((tm, D), lambda i: (i, 0)),
                pl.BlockSpec((tm, LANES), lambda i: (i % pos_tiles, 0)),
                pl.BlockSpec((tm, LANES), lambda i: (i % pos_tiles, 0))]
    in_specs += [pl.BlockSpec(w.shape, lambda i: (0, 0)) for w in weights]
    out_specs = [pl.BlockSpec((tm, w.shape[1]), lambda i: (i, 0)) for w in weights]
    out_shape = [jax.ShapeDtypeStruct((T, w.shape[1]), dt) for w, dt in zip(weights, dtypes)]
    return pl.pallas_call(
        functools.partial(_proj_kernel, kinds=tuple(kinds)),
        grid=(T // tm,), in_specs=in_specs, out_specs=out_specs, out_shape=out_shape,
        compiler_params=_cparams("parallel"), name="proj",
    )(x2d, cos2, sin2, *weights)


def _compress_kernel(r_ref, pe_ref, w1_ref, w2_ref, w2t_ref, o_ref, ot_ref):
    half = CMP_STRIDE * HEAD_DIM
    r = r_ref[0, 0]
    n = r.shape[0]
    top = _dot(r, w1_ref[0, :half, :])
    bot = _dot(r, w1_ref[0, half:, :])
    bias = _dot(pe_ref[0].astype(BF16), w1_ref[0])
    h = top + pltpu.roll(bot, n - 1, 0) + bias
    hb = _gelu_tanh(h).astype(BF16)
    o_ref[0, 0] = _dot(hb, w2_ref[0]).astype(o_ref.dtype)
    ot_ref[0, 0] = _dot_nt(w2t_ref[0], hb).astype(ot_ref.dtype)


def _nsa_compress(r, pe, w1, w2, w2t):
    _, bg, n, width = r.shape
    hid = w1.shape[2]
    return pl.pallas_call(
        _compress_kernel,
        grid=(2, bg),
        in_specs=[pl.BlockSpec((1, 1, n, width), lambda a, i: (a, i, 0, 0)),
                  pl.BlockSpec((1, 1, 2 * width), lambda a, i: (a, 0, 0)),
                  pl.BlockSpec((1, 2 * width, hid), lambda a, i: (a, 0, 0)),
                  pl.BlockSpec((1, hid, HEAD_DIM), lambda a, i: (a, 0, 0)),
                  pl.BlockSpec((1, HEAD_DIM, hid), lambda a, i: (a, 0, 0))],
        out_specs=[pl.BlockSpec((1, 1, n, HEAD_DIM), lambda a, i: (a, i, 0, 0)),
                   pl.BlockSpec((1, 1, HEAD_DIM, n), lambda a, i: (a, i, 0, 0))],
        out_shape=[jax.ShapeDtypeStruct((2, bg, n, HEAD_DIM), BF16),
                   jax.ShapeDtypeStruct((2, bg, HEAD_DIM, n), BF16)],
        compiler_params=_cparams("parallel", "parallel"), name="nsa_compress",
    )(r, pe, w1, w2, w2t)


def _exp_cols(s, ok):
    sm = jnp.where(ok, s, NEG)
    m = jnp.max(sm, axis=0, keepdims=True)
    p = jnp.exp(sm - jnp.where(m > 0.5 * NEG, m, 0.0))
    return p, jnp.sum(p, axis=0, keepdims=True)


def _nsa_attn_kernel(q_ref, kc_ref, vct_ref, ks_ref, vs_ref, kw_ref, vw_ref, gt_ref, c2st_ref, o_ref,
                     vst_ref, vwt_ref, sel_ref, s_ref, p_ref, *, seq, n_sel):
    Z, dh, QC, SB = NSA_HEADS_PER_GROUP, HEAD_DIM, NSA_QCHUNK, SEL_BLOCK
    W = Z * QC
    KT = 2 * SB
    ncp = seq // CMP_STRIDE
    nbs = seq // SEL_BLOCK
    c = pl.program_id(2)
    t0 = c * QC

    @pl.when(c == 0)
    def _():
        step = 4 * QC
        for r in range(seq // step):
            rows = slice(r * step, (r + 1) * step)
            vst_ref[0:dh, rows] = _transpose_bf16(vs_ref[0, 0, rows, :]).astype(BF16)
            vwt_ref[0:dh, rows] = _transpose_bf16(vw_ref[0, 0, rows, :]).astype(BF16)
        ones_row = jnp.where(lax.broadcasted_iota(I32, (VT_ROWS - dh, seq), 0) == 0, 1.0, 0.0).astype(BF16)
        vst_ref[dh:VT_ROWS, :] = ones_row
        vwt_ref[dh:VT_ROWS, :] = ones_row

    q_t = _transpose_bf16(q_ref[0]).astype(BF16)
    qt = jnp.concatenate([q_t[z * dh:(z + 1) * dh] for z in range(Z)], axis=1)
    qt = qt * jnp.asarray(dh ** -0.5, BF16)
    qpos = t0 + (lax.broadcasted_iota(I32, (1, W), 1) & (QC - 1))
    d0 = pl.multiple_of(t0, QC)
    band = QC + WINDOW
    w0 = pl.multiple_of(jnp.maximum(t0 - WINDOW, 0), QC)

    def tile4(row):
        return jnp.concatenate([row] * Z, axis=1)

    s_c = _dot(kc_ref[0, 0], qt)
    s_w = _dot(kw_ref[0, 0, pl.ds(w0, band), :], qt)
    s_d = _dot(ks_ref[0, 0, pl.ds(d0, KT), :], qt)
    s_ref[0] = _dot(ks_ref[0, 0, pl.ds(0, KT), :], qt)

    cmp_end = lax.broadcasted_iota(I32, (ncp, 1), 0) * CMP_STRIDE + (CMP_BLOCK - 1)
    e_c, l_c = _exp_cols(s_c, cmp_end <= qpos)
    p_c = e_c * (1.0 / jnp.maximum(l_c, 1e-30))
    o_c = _dot(vct_ref[0, 0], p_c.astype(BF16))

    p_sum = p_c[:, 0:QC]
    for z in range(1, Z):
        p_sum = p_sum + p_c[:, z * QC:(z + 1) * QC]
    p_hi, p_lo = _split(p_sum)
    imp = _dot(c2st_ref[...], p_hi) + _dot(c2st_ref[...], p_lo)

    jf = lax.broadcasted_iota(I32, (nbs, QC), 0).astype(F32)
    qblk = (qpos[:, :QC] // SEL_BLOCK).astype(F32)
    future = jf > qblk
    forced = (jf == 0.0) | (jf == qblk) | (jf == qblk - 1.0)
    score = jnp.where(future, -1.0, jnp.where(forced, 1e3, imp))
    selm = jnp.zeros((nbs, QC), F32)
    for _ in range(n_sel):
        m = jnp.max(score, axis=0, keepdims=True)
        idx = jnp.min(jnp.where(score == m, jf, float(nbs)), axis=0, keepdims=True)
        hit = jf == idx
        selm = jnp.where(hit, jnp.where(m >= 0.0, 1.0, 0.0), selm)
        score = jnp.where(hit, NEG, score)
    selw = tile4(selm)
    sel_ref[...] = jnp.where(lax.broadcasted_iota(I32, (nbs, W), 0) < 2 * c, selw, 0.0)

    brow = lax.broadcasted_iota(I32, (nbs, 1), 0)
    on_a = jnp.sum(jnp.where(brow == 2 * c, selw, 0.0), axis=0, keepdims=True)
    on_b = jnp.sum(jnp.where(brow == 2 * c + 1, selw, 0.0), axis=0, keepdims=True)
    krow = lax.broadcasted_iota(I32, (KT, 1), 0)
    ok_d = jnp.where(t0 + krow <= qpos, jnp.where(krow < SB, on_a, on_b), 0.0) > 0.5
    sm = jnp.where(ok_d, s_d, NEG)
    m0 = jnp.max(sm, axis=0, keepdims=True)
    p_d = jnp.exp(sm - m0)
    acc0 = _dot(vst_ref[:, pl.ds(d0, KT)], p_d.astype(BF16))
    p_ref[0] = jnp.zeros((KT, W), BF16)

    dist = qpos - (w0 + lax.broadcasted_iota(I32, (band, 1), 0))
    sm_w = jnp.where(jnp.where(dist >= 0, dist, WINDOW) < WINDOW, s_w, NEG)
    e_w = jnp.exp(sm_w - jnp.max(sm_w, axis=0, keepdims=True))
    acc_w = _dot(vwt_ref[:, pl.ds(w0, band)], e_w.astype(BF16))
    o_w = acc_w[0:dh] * (1.0 / jnp.maximum(acc_w[dh:dh + 1], 1e-30))

    def stage(t, src, dst, carry):
        m_i, acc = carry
        tp = jnp.maximum(t - 1, 0)
        k_next = pl.multiple_of(jnp.minimum(t + 1, c) * KT, KT)
        s_ref[dst] = _dot(ks_ref[0, 0, pl.ds(k_next, KT), :], qt)
        vt = vst_ref[:, pl.ds(pl.multiple_of(tp * KT, KT), KT)]
        pv_a = _dot(vt[:, :SB], p_ref[src, 0:SB, :])
        pv_b = _dot(vt[:, SB:], p_ref[src, SB:KT, :])
        on_a = sel_ref[pl.ds(2 * t, 1), :] > 0.5
        on_b = sel_ref[pl.ds(2 * t + 1, 1), :] > 0.5
        prev_a = sel_ref[pl.ds(2 * tp, 1), :] > 0.5
        prev_b = sel_ref[pl.ds(2 * tp + 1, 1), :] > 0.5
        mx_a = jnp.max(s_ref[src, 0:SB, :], axis=0, keepdims=True)
        mx_b = jnp.max(s_ref[src, SB:KT, :], axis=0, keepdims=True)
        m_new = jnp.maximum(m_i, jnp.maximum(jnp.where(on_a, mx_a, NEG), jnp.where(on_b, mx_b, NEG)))
        p_ref[dst] = jnp.exp(s_ref[src] - m_new).astype(BF16)
        acc_new = jnp.exp(m_i - m_new) * (acc + jnp.where(prev_a, pv_a, 0.0) + jnp.where(prev_b, pv_b, 0.0))
        return m_new, acc_new

    n_pairs = (c + 1) // 2
    _, acc_s = lax.fori_loop(0, n_pairs, lambda u, cr: stage(2 * u + 1, 1, 0, stage(2 * u, 0, 1, cr)), (m0, acc0))
    tl = jnp.maximum(2 * n_pairs - 1, 0)
    vt = vst_ref[:, pl.ds(pl.multiple_of(tl * KT, KT), KT)]
    last_a = sel_ref[pl.ds(2 * tl, 1), :] > 0.5
    last_b = sel_ref[pl.ds(2 * tl + 1, 1), :] > 0.5
    acc_s = (acc_s + jnp.where(last_a, _dot(vt[:, :SB], p_ref[0, 0:SB, :]), 0.0)
             + jnp.where(last_b, _dot(vt[:, SB:], p_ref[0, SB:KT, :]), 0.0))
    o_s = acc_s[0:dh] / jnp.maximum(acc_s[dh:dh + 1], 1e-30)

    gt = gt_ref[0, 0]
    g_c = jnp.concatenate([gt[3 * z:3 * z + 1] for z in range(Z)], axis=1)
    g_s = jnp.concatenate([gt[3 * z + 1:3 * z + 2] for z in range(Z)], axis=1)
    g_w = jnp.concatenate([gt[3 * z + 2:3 * z + 3] for z in range(Z)], axis=1)
    o_t = (g_c * o_c + g_s * o_s + g_w * o_w).astype(BF16)
    o_ref[0] = jnp.concatenate([_transpose_bf16(o_t[:, z * QC:(z + 1) * QC]) for z in range(Z)],
                               axis=1).astype(o_ref.dtype)


def _nsa_attention(q, kc, vct, ks, vs, kw, vw, gates_t, c2st, seq):
    B, G, _, dh = ks.shape
    Z, QC = NSA_HEADS_PER_GROUP, NSA_QCHUNK
    W = Z * QC
    KT = 2 * SEL_BLOCK
    ncp = seq // CMP_STRIDE
    nbs = seq // SEL_BLOCK
    full = lambda shp: pl.BlockSpec((1, 1) + shp, lambda b, g, c: (b, g, 0, 0))
    return pl.pallas_call(
        functools.partial(_nsa_attn_kernel, seq=seq, n_sel=min(SEL_TOPN, nbs)),
        grid=(B, G, seq // QC),
        in_specs=[pl.BlockSpec((1, QC, Z * dh), lambda b, g, c: (b, c, g)),
                  full((ncp, dh)), full((dh, ncp)), full((seq, dh)), full((seq, dh)), full((seq, dh)), full((seq, dh)),
                  pl.BlockSpec((1, 1, gates_t.shape[2], QC), lambda b, g, c: (b, g, 0, c)),
                  pl.BlockSpec((nbs, ncp), lambda b, g, c: (0, 0))],
        out_specs=pl.BlockSpec((1, QC, Z * dh), lambda b, g, c: (b, c, g)),
        out_shape=jax.ShapeDtypeStruct((B, seq, G * Z * dh), BF16),
        scratch_shapes=[pltpu.VMEM((VT_ROWS, seq), BF16), pltpu.VMEM((VT_ROWS, seq), BF16),
                        pltpu.VMEM((nbs, W), F32), pltpu.VMEM((2, KT, W), F32), pltpu.VMEM((2, KT, W), BF16)],
        compiler_params=_cparams("parallel", "parallel", "arbitrary"), name="nsa_attn",
    )(q, kc, vct, ks, vs, kw, vw, gates_t, c2st)


def _cmul(ar, ai, br, bi):
    return ar * br - ai * bi, ar * bi + ai * br


def _s5_ops_kernel(lre_ref, lim_ref, lstep_ref, bret_ref, bimt_ref, cre_ref, cim_ref,
                   mt_ref, pm_ref, qt_ref, al_ref):
    L, C, P = S5_CHUNK, S5_GROUP_CH, S5_STATE
    lr = lre_ref[0]
    li = lim_ref[0]
    step = jnp.exp(lstep_ref[0])
    mag = jnp.exp(lr * step)
    ar = mag * jnp.cos(li * step)
    ai = mag * jnp.sin(li * step)
    den = lr * lr + li * li
    fr = ((ar - 1.0) * lr + ai * li) / den
    fi = (ai * lr - (ar - 1.0) * li) / den
    bbr, bbi = _cmul(fr, fi, bret_ref[0], bimt_ref[0])

    up_r, up_i = jnp.ones((1, 1, P), F32), jnp.zeros((1, 1, P), F32)
    dn_r, dn_i = up_r, up_i
    sr, si = ar.reshape(1, 1, P), ai.reshape(1, 1, P)
    a1r, a1i = sr, si
    n = 1
    while n < L:
        tr, ti = _cmul(up_r, up_i, sr, si)
        up_r, up_i = jnp.concatenate([up_r, tr], 0), jnp.concatenate([up_i, ti], 0)
        tr, ti = _cmul(dn_r, dn_i, sr, si)
        dn_r, dn_i = jnp.concatenate([tr, dn_r], 0), jnp.concatenate([ti, dn_i], 0)
        sr, si = _cmul(sr, si, sr, si)
        n *= 2
    al_ref[0] = jnp.concatenate([sr.reshape(1, P), si.reshape(1, P)], -1)

    cr = cre_ref[0][None]
    ci = cim_ref[0][None]
    nr, ni = _cmul(up_r, up_i, a1r, a1i)
    wr, wi = _cmul(cr, ci, nr, ni)
    qt_ref[0] = jnp.concatenate([wr.reshape(L * C, P), -wi.reshape(L * C, P)], -1)
    er, ei = _cmul(dn_r, dn_i, bbr[None], bbi[None])
    pm_ref[0] = jnp.concatenate([er.reshape(L * C, P), ei.reshape(L * C, P)], -1).astype(pm_ref.dtype)
    w0r, w0i = _cmul(cr, ci, up_r, up_i)
    kt = _dot3_nt(bbr, w0r.reshape(L * C, P)) - _dot3_nt(bbi, w0i.reshape(L * C, P))
    lane = lax.broadcasted_iota(I32, kt.shape, 1)
    mt_ref[0, 0:C, :] = kt.astype(mt_ref.dtype)
    for s in range(1, L):
        shifted = jnp.where(lane >= s * C, pltpu.roll(kt, s * C, 1), 0.0)
        mt_ref[0, s * C:(s + 1) * C, :] = shifted.astype(mt_ref.dtype)


def _s5_operators(lre, lim, lstep, bret, bimt, cre, cim):
    G = lre.shape[0]
    L, C, P = S5_CHUNK, S5_GROUP_CH, S5_STATE
    vec = pl.BlockSpec((1, 1, P), lambda g: (g, 0, 0))
    mat = pl.BlockSpec((1, C, P), lambda g: (g, 0, 0))
    return pl.pallas_call(
        _s5_ops_kernel,
        grid=(G,),
        in_specs=[vec, vec, pl.BlockSpec((1, 1, 1), lambda g: (g, 0, 0)), mat, mat, mat, mat],
        out_specs=[pl.BlockSpec((1, L * C, L * C), lambda g: (g, 0, 0)),
                   pl.BlockSpec((1, L * C, 2 * P), lambda g: (g, 0, 0)),
                   pl.BlockSpec((1, L * C, 2 * P), lambda g: (g, 0, 0)),
                   pl.BlockSpec((1, 1, 2 * P), lambda g: (g, 0, 0))],
        out_shape=[jax.ShapeDtypeStruct((G, L * C, L * C), BF16),
                   jax.ShapeDtypeStruct((G, L * C, 2 * P), BF16),
                   jax.ShapeDtypeStruct((G, L * C, 2 * P), F32),
                   jax.ShapeDtypeStruct((G, 1, 2 * P), F32)],
        compiler_params=_cparams("parallel"), name="s5_operators",
    )(lre, lim, lstep, bret, bimt, cre, cim)


def _s5_scan_kernel(u_ref, mt_ref, pm_ref, qt_ref, al_ref, d_ref, y_ref, xin_ref, *, batch):
    P = S5_STATE
    rows = u_ref.shape[1]
    u = u_ref[0]
    ub = u.astype(BF16)
    y_ref[0] = _dot(ub, mt_ref[0]) + d_ref[0] * u
    xloc = _dot(ub, pm_ref[0])
    al = al_ref[0]
    lane = lax.broadcasted_iota(I32, (1, 2 * P), 1)
    mul_same = jnp.where(lane < P, al, pltpu.roll(al, P, 1))
    mul_swap = jnp.where(lane < P, -pltpu.roll(al, P, 1), al)
    xin_ref[...] = xloc

    def carry_step(k, state):
        r0 = pl.multiple_of(k * batch, batch)
        loc = xin_ref[pl.ds(r0, batch), :]
        xin_ref[pl.ds(r0, batch), :] = state
        return state * mul_same + pltpu.roll(state, P, 1) * mul_swap + loc

    lax.fori_loop(0, rows // batch, carry_step, jnp.zeros((batch, 2 * P), F32))
    y_ref[0] += _dot3_nt(xin_ref[...], qt_ref[0])


def _s5_scan(u, mt, pm, qt, al, d, batch):
    G, rows, width = u.shape
    P2 = 2 * S5_STATE
    per_g = lambda shp: pl.BlockSpec((1,) + shp, lambda g: (g, 0, 0))
    return pl.pallas_call(
        functools.partial(_s5_scan_kernel, batch=batch),
        grid=(G,),
        in_specs=[per_g((rows, width)), per_g((width, width)), per_g((width, P2)), per_g((width, P2)),
                  per_g((1, P2)), per_g((1, width))],
        out_specs=per_g((rows, width)),
        out_shape=jax.ShapeDtypeStruct((G, rows, width), F32),
        scratch_shapes=[pltpu.VMEM((rows, P2), F32)],
        compiler_params=_cparams("parallel"), name="s5_scan",
    )(u, mt, pm, qt, al, d)


def _even_out_kernel(x_ref, nsa_ref, y_ref, gw_ref, gb_ref, wa_ref, wb_ref, lg_ref, lb_ref, o_ref, *, alpha):
    g = _gelu_tanh(y_ref[...])
    s5 = g * _sigmoid(_dot(g.astype(BF16), gw_ref[...]) + gb_ref[...])
    mix = _dot(nsa_ref[...], wa_ref[...]) + _dot(s5.astype(BF16), wb_ref[...])
    o_ref[...] = _layer_norm(alpha * x_ref[...] + mix, lg_ref[...], lb_ref[...])


def _even_out(x2d, o_nsa, y_s5, glu_w, glu_b, wa, wb, ln_g, ln_b, alpha):
    T, D = x2d.shape
    tm = min(ROW_TILE, T)
    row = lambda w: pl.BlockSpec((tm, w), lambda i: (i, 0))
    whole = lambda a: pl.BlockSpec(a.shape, lambda i: (0, 0))
    return pl.pallas_call(
        functools.partial(_even_out_kernel, alpha=alpha),
        grid=(T // tm,),
        in_specs=[row(D), row(o_nsa.shape[1]), row(y_s5.shape[1]), whole(glu_w), whole(glu_b), whole(wa), whole(wb),
                  whole(ln_g), whole(ln_b)],
        out_specs=row(D), out_shape=jax.ShapeDtypeStruct((T, D), F32),
        compiler_params=_cparams("parallel"), name="even_out",
    )(x2d, o_nsa, y_s5, glu_w, glu_b, wa, wb, ln_g, ln_b)


def _odd_out_kernel(x_ref, at_ref, w_ref, lg_ref, lb_ref, o_ref, *, alpha):
    attn = _transpose_bf16(at_ref[0]).astype(BF16)
    mix = _dot(attn, w_ref[...])
    o_ref[...] = _layer_norm(alpha * x_ref[...] + mix, lg_ref[...], lb_ref[...])


def _odd_out(x2d, attn_t, w, ln_g, ln_b, alpha):
    T, D = x2d.shape
    _, width, seq = attn_t.shape
    tm = min(ROW_TILE, seq)
    tiles = seq // tm
    row = lambda wd: pl.BlockSpec((tm, wd), lambda i: (i, 0))
    whole = lambda a: pl.BlockSpec(a.shape, lambda i: (0, 0))
    return pl.pallas_call(
        functools.partial(_odd_out_kernel, alpha=alpha),
        grid=(T // tm,),
        in_specs=[row(D), pl.BlockSpec((1, width, tm), lambda i: (i // tiles, 0, i % tiles)),
                  whole(w), whole(ln_g), whole(ln_b)],
        out_specs=row(D), out_shape=jax.ShapeDtypeStruct((T, D), F32),
        compiler_params=_cparams("parallel"), name="odd_out",
    )(x2d, attn_t, w, ln_g, ln_b)


def _eye(n):
    return (lax.broadcasted_iota(I32, (n, n), 0) == lax.broadcasted_iota(I32, (n, n), 1)).astype(BF16)


def _transpose_bf16(a):
    return _dot_nt(_eye(a.shape[1]), a)


def _moba_kernel(q_ref, k_ref, v_ref, o_ref, kmean_ref, vt_ref, sel_ref, s_ref, p_ref, *, seq, top):
    BLK, dh, NH = MOBA_BLOCK, HEAD_DIM, MOBA_HEADS_PER_STEP
    nb = seq // BLK
    i = pl.program_id(2)

    @pl.when(i == 0)
    def _():
        kmean_ref[...] = jnp.mean(k_ref[0].astype(F32).reshape(nb, BLK, NH * dh), axis=1)
        for r in range(nb):
            cols = slice(r * BLK, (r + 1) * BLK)
            v_t = _transpose_bf16(v_ref[0, cols, :]).astype(BF16)
            for h in range(NH):
                vt_ref[h, 0:dh, cols] = v_t[h * dh:(h + 1) * dh]
        ones_row = jnp.where(lax.broadcasted_iota(I32, (VT_ROWS - dh, seq), 0) == 0, 1.0, 0.0).astype(BF16)
        for h in range(NH):
            vt_ref[h, dh:VT_ROWS, :] = ones_row

    jf = lax.broadcasted_iota(I32, (nb, BLK), 0).astype(F32)
    i_f = i.astype(F32)
    causal = lax.broadcasted_iota(I32, (BLK, BLK), 0) <= lax.broadcasted_iota(I32, (BLK, BLK), 1)

    heads = tuple(slice(h * dh, (h + 1) * dh) for h in range(NH))
    qs = [q_ref[0, :, hs] * jnp.asarray(dh ** -0.5, BF16) for hs in heads]

    def scores(j, h):
        c0 = pl.multiple_of(j * BLK, BLK)
        return _dot_nt(k_ref[0, pl.ds(c0, BLK), heads[h]], qs[h])

    def values(j, h, p):
        c0 = pl.multiple_of(j * BLK, BLK)
        return _dot(vt_ref[h, :, pl.ds(c0, BLK)], p)

    gates = []
    for h in range(NH):
        km_hi, km_lo = _split(kmean_ref[:, heads[h]])
        gates.append(_dot_nt(km_hi, qs[h]) + _dot_nt(km_lo, qs[h]))
    s_own = [scores(i, h) for h in range(NH)]
    s_first = [scores(0, h) for h in range(NH)]

    start = []
    for h in range(NH):
        gate = jnp.where(jf < i_f, gates[h], NEG)
        selm = jnp.zeros((nb, BLK), F32)
        for _ in range(top):
            m = jnp.max(gate, axis=0, keepdims=True)
            idx = jnp.min(jnp.where(gate == m, jf, float(nb)), axis=0, keepdims=True)
            hit = jf == idx
            selm = jnp.where(hit, jnp.where(m > 0.5 * NEG, 1.0, 0.0), selm)
            gate = jnp.where(hit, 2.0 * NEG, gate)
        sel_ref[h] = jnp.concatenate([selm, jnp.ones((SUBLANES, BLK), F32)], axis=0)
        sm = jnp.where(causal, s_own[h], NEG)
        m0 = jnp.max(sm, axis=0, keepdims=True)
        p = jnp.exp(sm - m0)
        s_ref[0, h] = s_first[h]
        p_ref[0, h] = p.astype(BF16)
        start += [m0, jnp.zeros((VT_ROWS, BLK), F32)]

    def prev_of(t):
        return jnp.where(t == 0, i, t - 1), jnp.where(t == 0, nb, t - 1)

    def stage(t, src, dst, carry):
        blk_prev, row_prev = prev_of(t)
        nxt = jnp.minimum(t + 1, nb - 1)
        for h in range(NH):
            s_ref[dst, h] = scores(nxt, h)
        pv = [values(blk_prev, h, p_ref[src, h]) for h in range(NH)]
        out = []
        for h in range(NH):
            m_i, acc = carry[2 * h:2 * h + 2]
            on = sel_ref[h, pl.ds(t, 1), :] > 0.5
            on_prev = sel_ref[h, pl.ds(row_prev, 1), :] > 0.5
            m_new = jnp.where(on, jnp.maximum(m_i, jnp.max(s_ref[src, h], axis=0, keepdims=True)), m_i)
            p_ref[dst, h] = jnp.exp(s_ref[src, h] - m_new).astype(BF16)
            out += [m_new, jnp.exp(m_i - m_new) * (acc + jnp.where(on_prev, pv[h], 0.0))]
        return tuple(out)

    n_pairs = (i + 1) // 2
    fin = lax.fori_loop(0, n_pairs, lambda u, c: stage(2 * u + 1, 1, 0, stage(2 * u, 0, 1, c)), tuple(start))
    blk_prev, row_prev = prev_of(2 * n_pairs)
    outs = []
    for h in range(NH):
        on_prev = sel_ref[h, pl.ds(row_prev, 1), :] > 0.5
        acc = fin[2 * h + 1] + jnp.where(on_prev, values(blk_prev, h, p_ref[0, h]), 0.0)
        outs.append(acc[0:dh] / acc[dh:dh + 1])
    o_ref[0] = jnp.concatenate(outs, axis=0).astype(o_ref.dtype)


def _moba_attention(q, k, v, seq):
    B, _, width = q.shape
    BLK, NH = MOBA_BLOCK, MOBA_HEADS_PER_STEP
    hw = NH * HEAD_DIM
    nb = seq // BLK
    top = min(MOBA_TOPK, max(nb - 1, 1))
    return pl.pallas_call(
        functools.partial(_moba_kernel, seq=seq, top=top),
        grid=(B, width // hw, nb),
        in_specs=[pl.BlockSpec((1, BLK, hw), lambda b, h, i: (b, i, h)),
                  pl.BlockSpec((1, seq, hw), lambda b, h, i: (b, 0, h)),
                  pl.BlockSpec((1, seq, hw), lambda b, h, i: (b, 0, h))],
        out_specs=pl.BlockSpec((1, hw, BLK), lambda b, h, i: (b, h, i)),
        out_shape=jax.ShapeDtypeStruct((B, width, seq), BF16),
        scratch_shapes=[pltpu.VMEM((nb, hw), F32), pltpu.VMEM((NH, VT_ROWS, seq), BF16),
                        pltpu.VMEM((NH, nb + SUBLANES, BLK), F32),
                        pltpu.VMEM((2, NH, BLK, BLK), F32), pltpu.VMEM((2, NH, BLK, BLK), BF16)],
        compiler_params=_cparams("parallel", "parallel", "arbitrary"), name="moba_attn",
    )(q, k, v)


def _route(logits):
    n_fine = MOE_GROUPS * MOE_EXPERTS_PER_GROUP
    lane = lax.broadcasted_iota(I32, logits.shape, 1)
    lf = lane.astype(F32)
    is_coarse = (lane >= n_fine) & (lane < n_fine + MOE_GROUPS)
    pc = _masked_softmax(logits, is_coarse)
    gv = jnp.max(pc, axis=-1, keepdims=True)
    gidx = jnp.min(jnp.where(is_coarse & (pc == gv), lf - float(n_fine), float(MOE_GROUPS)), axis=-1, keepdims=True)
    in_group = (lane < n_fine) & ((lane // MOE_EXPERTS_PER_GROUP).astype(F32) == gidx)
    pf = _masked_softmax(logits, in_group)
    cand = jnp.where(in_group, pf, -1.0)
    m1 = jnp.max(cand, axis=-1, keepdims=True)
    i1 = jnp.min(jnp.where(cand == m1, lf, float(LANES)), axis=-1, keepdims=True)
    cand = jnp.where(lf == i1, -1.0, cand)
    m2 = jnp.max(cand, axis=-1, keepdims=True)
    i2 = jnp.min(jnp.where(cand == m2, lf, float(LANES)), axis=-1, keepdims=True)
    tot = m1 + m2
    return jnp.where(lf == i1, gv * (m1 / tot), jnp.where(lf == i2, gv * (m2 / tot), 0.0))


def _moe_kernel(x_ref, wr_ref, br_ref, wg_ref, wu_ref, wd_ref, lg_ref, lb_ref, o_ref, gates_ref, acc_ref, *, alpha):
    E, FF = MOE_EXPERTS_PER_GROUP, MOE_FF
    g = pl.program_id(1)
    x = x_ref[...]

    @pl.when(g == 0)
    def _():
        gates_ref[...] = _route(_dot3(x, wr_ref[...]) + br_ref[...])
        acc_ref[...] = jnp.zeros_like(acc_ref)

    xb = x.astype(BF16)
    hg = _dot(xb, wg_ref[0])
    hu = _dot(xb, wu_ref[0])
    h = hg * _sigmoid(hg) * hu
    gates = gates_ref[...]
    lane = lax.broadcasted_iota(I32, gates.shape, 1)
    parts = []
    for e in range(E):
        ge = jnp.sum(jnp.where(lane == g * E + e, gates, 0.0), axis=-1, keepdims=True)
        parts.append((h[:, e * FF:(e + 1) * FF] * ge).astype(BF16))
    acc_ref[...] += _dot(jnp.concatenate(parts, axis=-1), wd_ref[0])

    @pl.when(g == pl.num_programs(1) - 1)
    def _():
        o_ref[...] = _layer_norm(alpha * x + acc_ref[...], lg_ref[...], lb_ref[...])


def _moe(x2d, wr, br, wg, wu, wd, ln_g, ln_b, alpha):
    T, D = x2d.shape
    tm = min(ROW_TILE, T)
    NG = wg.shape[0]
    whole = lambda a: pl.BlockSpec(a.shape, lambda i, g: (0, 0))
    per_g = lambda a: pl.BlockSpec((1,) + a.shape[1:], lambda i, g: (g, 0, 0))
    return pl.pallas_call(
        functools.partial(_moe_kernel, alpha=alpha),
        grid=(T // tm, NG),
        in_specs=[pl.BlockSpec((tm, D), lambda i, g: (i, 0)), whole(wr), whole(br), per_g(wg), per_g(wu), per_g(wd),
                  whole(ln_g), whole(ln_b)],
        out_specs=pl.BlockSpec((tm, D), lambda i, g: (i, 0)),
        out_shape=jax.ShapeDtypeStruct((T, D), F32),
        scratch_shapes=[pltpu.VMEM((tm, LANES), F32), pltpu.VMEM((tm, D), F32)],
        compiler_params=_cparams("parallel", "arbitrary"), name="moe",
    )(x2d, wr, br, wg, wu, wd, ln_g, ln_b)


def _rope_tables(seq):
    inv = 1.0 / (ROPE_THETA ** (jnp.arange(0, HEAD_DIM, 2, dtype=F32) / HEAD_DIM))
    ang = jnp.arange(seq, dtype=F32)[:, None] * inv[None, :]
    cos, sin = jnp.cos(ang), jnp.sin(ang)
    reps = LANES // HEAD_DIM
    return jnp.tile(jnp.concatenate([cos, cos], -1), (1, reps)), jnp.tile(jnp.concatenate([-sin, sin], -1), (1, reps))


def _selection_constants(seq):
    ncp = seq // CMP_STRIDE
    nbs = seq // SEL_BLOCK
    starts = jnp.arange(ncp) * CMP_STRIDE
    bstart = jnp.arange(nbs) * SEL_BLOCK
    ovl = jnp.clip(jnp.minimum(starts[:, None] + CMP_BLOCK, bstart[None, :] + SEL_BLOCK)
                   - jnp.maximum(starts[:, None], bstart[None, :]), 0, CMP_BLOCK)
    return (ovl.astype(F32) / CMP_BLOCK).astype(BF16).T


def _moe_weights(w_coarse, b_coarse, w_fine, b_fine, w_gate, w_up, w_down):
    D = w_coarse.shape[0]
    NG, E, _, FF = w_gate.shape
    n_fine = NG * E
    wr = jnp.zeros((D, LANES), F32)
    wr = wr.at[:, :n_fine].set(w_fine.transpose(1, 0, 2).reshape(D, n_fine)).at[:, n_fine:n_fine + NG].set(w_coarse)
    br = jnp.zeros((1, LANES), F32)
    br = br.at[0, :n_fine].set(b_fine.reshape(n_fine)).at[0, n_fine:n_fine + NG].set(b_coarse)
    wg = w_gate.transpose(0, 2, 1, 3).reshape(NG, D, E * FF).astype(BF16)
    wu = w_up.transpose(0, 2, 1, 3).reshape(NG, D, E * FF).astype(BF16)
    wd = w_down.reshape(NG, E * FF, D).astype(BF16)
    return wr, br, wg, wu, wd


def _even_mixer(x2d, B, S, cos2, sin2, w_in, pe_k, pe_v, k_w1, k_w2, v_w1, v_w2,
                lam_re, lam_im, log_step, b_re, b_im, c_re, c_im, d_skip, glu_w, glu_b, w_out, ln_g, ln_b, alpha):
    G, Z, dh = NSA_KV_GROUPS, NSA_HEADS_PER_GROUP, HEAD_DIM
    kvw = NSA_KV_WIDTH
    D = x2d.shape[1]
    cuts = [NSA_WIDTH + i * kvw for i in range(7)]
    w_q, w_kc, w_vc, w_ks, w_vs, w_kw, w_vw = [w_in[:, a:b] for a, b in zip([0] + cuts[:-1], cuts)]
    n_gate = 3 * NSA_HEADS
    w_g = jnp.pad(w_in[:, cuts[-1]:cuts[-1] + n_gate], ((0, 0), (0, LANES - n_gate)))
    w_u = w_in[:, cuts[-1] + n_gate:]
    weights = [w_q.astype(BF16), jnp.concatenate([w_kc, w_ks, w_kw], 1).astype(BF16),
               jnp.concatenate([w_vc, w_vs, w_vw], 1).astype(BF16), w_u.astype(BF16), w_g.astype(BF16)]
    q, kk, vv, u, gates = _project(x2d, cos2, sin2, weights, ["rope", "rope", "plain", "plain", "sigmoid"],
                                   [BF16, BF16, BF16, F32, F32], S)

    def heads(t):
        return t.reshape(B, S, G, dh).transpose(0, 2, 1, 3)

    kc, ks, kw = [heads(kk[:, i * kvw:(i + 1) * kvw]) for i in range(3)]
    vc, vs, vw = [heads(vv[:, i * kvw:(i + 1) * kvw]) for i in range(3)]
    gates_t = gates[:, :n_gate].reshape(B, S, G, 3 * Z).transpose(0, 2, 3, 1)

    n_rows = S // CMP_STRIDE
    r = jnp.stack([kc, vc]).reshape(2, B * G, n_rows, CMP_STRIDE * dh)
    pe = jnp.stack([pe_k, pe_v]).reshape(2, 1, CMP_BLOCK * dh)
    w1 = jnp.stack([k_w1, v_w1]).astype(BF16)
    w2 = jnp.stack([k_w2, v_w2]).astype(BF16)
    cmp, cmp_t = _nsa_compress(r, pe, w1, w2, w2.transpose(0, 2, 1))
    kcm = cmp[0].reshape(B, G, n_rows, dh)
    vct = cmp_t[1].reshape(B, G, dh, n_rows)

    o_nsa = _nsa_attention(q.reshape(B, S, NSA_WIDTH), kcm, vct, ks, vs, kw, vw, gates_t, _selection_constants(S), S)

    C, L = S5_GROUP_CH, S5_CHUNK
    SG = u.shape[1] // C
    mt, pm, qt, al = _s5_operators(lam_re[:, None, :], lam_im[:, None, :], log_step[:, None, None],
                                   b_re.transpose(0, 2, 1), b_im.transpose(0, 2, 1), c_re, c_im)
    u_g = u.reshape(B, S // L, L, SG, C).transpose(3, 1, 0, 2, 4).reshape(SG, (S // L) * B, L * C)
    d_t = jnp.tile(d_skip.reshape(SG, 1, C), (1, 1, L))
    y = _s5_scan(u_g, mt, pm, qt, al, d_t, B)
    y = y.reshape(SG, S // L, B, L, C).transpose(2, 1, 3, 0, 4).reshape(B * S, SG * C)

    return _even_out(x2d, o_nsa.reshape(B * S, NSA_WIDTH), y, glu_w.astype(BF16), glu_b[None, :],
                     w_out[:NSA_WIDTH].astype(BF16), w_out[NSA_WIDTH:].astype(BF16), ln_g[None, :], ln_b[None, :], alpha)


def _odd_mixer(x2d, B, S, cos2, sin2, w_in, w_out, ln_g, ln_b, alpha):
    D = x2d.shape[1]
    q, k, v = _project(x2d, cos2, sin2, [w_in[:, i * D:(i + 1) * D].astype(BF16) for i in range(3)],
                       ["rope", "rope", "plain"], [BF16, BF16, BF16], S)
    attn_t = _moba_attention(q.reshape(B, S, D), k.reshape(B, S, D), v.reshape(B, S, D), S)
    return _odd_out(x2d, attn_t, w_out.astype(BF16), ln_g[None, :], ln_b[None, :], alpha)


def kernel(x, ev_w_in, nsa_pe_k, nsa_pe_v, nsa_cmp_k_w1, nsa_cmp_k_w2, nsa_cmp_v_w1, nsa_cmp_v_w2, s5_lambda_re, s5_lambda_im, s5_log_step, s5_b_re, s5_b_im, s5_c_re, s5_c_im, s5_d, s5_glu_w, s5_glu_b, ev_w_out, od_w_in, od_w_out, ln_mix_g, ln_mix_b, ln_ffn_g, ln_ffn_b, moe_w_coarse, moe_b_coarse, moe_w_fine, moe_b_fine, moe_w_gate, moe_w_up, moe_w_down):
    B, S, D = x.shape
    depth = ln_mix_g.shape[0]
    alpha = (2.0 * depth) ** 0.25
    cos2, sin2 = _rope_tables(S)
    h = x.reshape(B * S, D)
    for layer in range(depth):
        if layer % 2 == 0:
            e = layer // 2
            h = _even_mixer(h, B, S, cos2, sin2, ev_w_in[e], nsa_pe_k[e], nsa_pe_v[e], nsa_cmp_k_w1[e], nsa_cmp_k_w2[e],
                            nsa_cmp_v_w1[e], nsa_cmp_v_w2[e], s5_lambda_re[e], s5_lambda_im[e], s5_log_step[e],
                            s5_b_re[e], s5_b_im[e], s5_c_re[e], s5_c_im[e], s5_d[e], s5_glu_w[e], s5_glu_b[e],
                            ev_w_out[e], ln_mix_g[layer], ln_mix_b[layer], alpha)
        else:
            o = layer // 2
            h = _odd_mixer(h, B, S, cos2, sin2, od_w_in[o], od_w_out[o], ln_mix_g[layer], ln_mix_b[layer], alpha)
        wr, br, wg, wu, wd = _moe_weights(moe_w_coarse[layer], moe_b_coarse[layer], moe_w_fine[layer], moe_b_fine[layer],
                                          moe_w_gate[layer], moe_w_up[layer], moe_w_down[layer])
        h = _moe(h, wr, br, wg, wu, wd, ln_ffn_g[layer][None, :], ln_ffn_b[layer][None, :], alpha)
    return h.reshape(B, S, D)
```

```python
import functools
import math

import jax
import jax.numpy as jnp
from jax import lax
from jax.experimental import pallas as pl
from jax.experimental.pallas import tpu as pltpu

F32 = jnp.float32
BF16 = jnp.bfloat16
I32 = jnp.int32

HEAD_DIM = 64
ROPE_THETA = 10000.0
LN_EPS = 1e-5

NSA_KV_GROUPS = 2
NSA_HEADS_PER_GROUP = 4
NSA_HEADS = NSA_KV_GROUPS * NSA_HEADS_PER_GROUP
NSA_WIDTH = NSA_HEADS * HEAD_DIM
NSA_KV_WIDTH = NSA_KV_GROUPS * HEAD_DIM
CMP_BLOCK = 32
CMP_STRIDE = 16
SEL_BLOCK = 64
SEL_TOPN = 8
WINDOW = 256
NSA_QCHUNK = 128

S5_GROUP_CH = 16
S5_STATE = 64
S5_CHUNK = 64

MOBA_BLOCK = 256
MOBA_TOPK = 3
MOBA_HEADS_PER_STEP = 8

MOE_GROUPS = 4
MOE_EXPERTS_PER_GROUP = 8
MOE_FF = 128

LANES = 128
SUBLANES = 8
MXU_COLS = 256
VT_ROWS = HEAD_DIM + 16
ROW_TILE = 512
VMEM_LIMIT = 56 * 1024 * 1024
NEG = -1e30


def _cparams(*sem):
    return pltpu.CompilerParams(dimension_semantics=sem, vmem_limit_bytes=VMEM_LIMIT)


def _dot(a, b):
    return jnp.dot(a, b, preferred_element_type=F32)


def _dot_nt(a, b):
    return lax.dot_general(a, b, (((1,), (1,)), ((), ())), preferred_element_type=F32)


def _split(a):
    hi = a.astype(BF16)
    lo = (a - hi.astype(F32)).astype(BF16)
    return hi, lo


def _dot3(a, b):
    ah, al = _split(a)
    bh, bl = _split(b)
    return _dot(ah, bh) + _dot(ah, bl) + _dot(al, bh)


def _dot3_nt(a, b):
    ah, al = _split(a)
    bh, bl = _split(b)
    return _dot_nt(ah, bh) + _dot_nt(ah, bl) + _dot_nt(al, bh)


def _sigmoid(x):
    return 1.0 / (1.0 + jnp.exp(-x))


def _gelu_tanh(x):
    return 0.5 * x * (1.0 + jnp.tanh(math.sqrt(2.0 / math.pi) * (x + 0.044715 * (x * x * x))))


def _layer_norm(r, g, b):
    mu = jnp.mean(r, axis=-1, keepdims=True)
    d = r - mu
    var = jnp.mean(d * d, axis=-1, keepdims=True)
    return d * lax.rsqrt(var + LN_EPS) * g + b


def _masked_softmax(s, ok):
    sm = jnp.where(ok, s, NEG)
    m = jnp.max(sm, axis=-1, keepdims=True)
    p = jnp.where(ok, jnp.exp(sm - m), 0.0)
    return p / jnp.maximum(jnp.sum(p, axis=-1, keepdims=True), 1e-30)


def _rope_128(a, cos2, sin2):
    lane = lax.broadcasted_iota(I32, a.shape, 1)
    first_half = (lane & (HEAD_DIM - 1)) < (HEAD_DIM // 2)
    partner = jnp.where(first_half, pltpu.roll(a, LANES - HEAD_DIM // 2, 1), pltpu.roll(a, HEAD_DIM // 2, 1))
    return a * cos2 + partner * sin2


def _proj_kernel(x_ref, cos_ref, sin_ref, *refs, kinds):
    n = len(kinds)
    w_refs, o_refs = refs[:n], refs[n:]
    xb = x_ref[...].astype(BF16)
    cos2 = cos_ref[...]
    sin2 = sin_ref[...]
    for kind, w_ref, o_ref in zip(kinds, w_refs, o_refs):
        width = w_ref.shape[1]
        for c0 in range(0, width, MXU_COLS):
            cw = min(MXU_COLS, width - c0)
            acc = _dot(xb, w_ref[:, c0:c0 + cw])
            for l0 in range(0, cw, LANES):
                part = acc[:, l0:l0 + LANES]
                if kind == "rope":
                    part = _rope_128(part, cos2, sin2)
                elif kind == "sigmoid":
                    part = _sigmoid(part)
                o_ref[:, c0 + l0:c0 + l0 + LANES] = part.astype(o_ref.dtype)


def _project(x2d, cos2, sin2, weights, kinds, dtypes, seq):
    T, D = x2d.shape
    tm = min(ROW_TILE, seq)
    pos_tiles = seq // tm
    in_specs = [pl.BlockSpec((tm, D), lambda i: (i, 0)),
                pl.BlockSpec((tm, LANES), lambda i: (i % pos_tiles, 0)),
                pl.BlockSpec((tm, LANES), lambda i: (i % pos_tiles, 0))]
    in_specs += [pl.BlockSpec(w.shape, lambda i: (0, 0)) for w in weights]
    out_specs = [pl.BlockSpec((tm, w.shape[1]), lambda i: (i, 0)) for w in weights]
    out_shape = [jax.ShapeDtypeStruct((T, w.shape[1]), dt) for w, dt in zip(weights, dtypes)]
    return pl.pallas_call(
        functools.partial(_proj_kernel, kinds=tuple(kinds)),
        grid=(T // tm,), in_specs=in_specs, out_specs=out_specs, out_shape=out_shape,
        compiler_params=_cparams("parallel"), name="proj",
    )(x2d, cos2, sin2, *weights)


def _compress_kernel(r_ref, pe_ref, w1_ref, w2_ref, w2t_ref, o_ref, ot_ref):
    half = CMP_STRIDE * HEAD_DIM
    r = r_ref[0, 0]
    n = r.shape[0]
    top = _dot(r, w1_ref[0, :half, :])
    bot = _dot(r, w1_ref[0, half:, :])
    bias = _dot(pe_ref[0].astype(BF16), w1_ref[0])
    h = top + pltpu.roll(bot, n - 1, 0) + bias
    hb = _gelu_tanh(h).astype(BF16)
    o_ref[0, 0] = _dot(hb, w2_ref[0]).astype(o_ref.dtype)
    ot_ref[0, 0] = _dot_nt(w2t_ref[0], hb).astype(ot_ref.dtype)


def _nsa_compress(r, pe, w1, w2, w2t):
    _, bg, n, width = r.shape
    hid = w1.shape[2]
    return pl.pallas_call(
        _compress_kernel,
        grid=(2, bg),
        in_specs=[pl.BlockSpec((1, 1, n, width), lambda a, i: (a, i, 0, 0)),
                  pl.BlockSpec((1, 1, 2 * width), lambda a, i: (a, 0, 0)),
                  pl.BlockSpec((1, 2 * width, hid), lambda a, i: (a, 0, 0)),
                  pl.BlockSpec((1, hid, HEAD_DIM), lambda a, i: (a, 0, 0)),
                  pl.BlockSpec((1, HEAD_DIM, hid), lambda a, i: (a, 0, 0))],
        out_specs=[pl.BlockSpec((1, 1, n, HEAD_DIM), lambda a, i: (a, i, 0, 0)),
                   pl.BlockSpec((1, 1, HEAD_DIM, n), lambda a, i: (a, i, 0, 0))],
        out_shape=[jax.ShapeDtypeStruct((2, bg, n, HEAD_DIM), BF16),
                   jax.ShapeDtypeStruct((2, bg, HEAD_DIM, n), BF16)],
        compiler_params=_cparams("parallel", "parallel"), name="nsa_compress",
    )(r, pe, w1, w2, w2t)


def _exp_cols(s, ok):
    sm = jnp.where(ok, s, NEG)
    m = jnp.max(sm, axis=0, keepdims=True)
    p = jnp.exp(sm - jnp.where(m > 0.5 * NEG, m, 0.0))
    return p, jnp.sum(p, axis=0, keepdims=True)


def _nsa_attn_kernel(q_ref, kc_ref, vct_ref, ks_ref, vs_ref, kw_ref, vw_ref, gt_ref, c2st_ref, o_ref,
                     vst_ref, vwt_ref, sel_ref, s_ref, p_ref, *, seq, n_sel):
    Z, dh, QC, SB = NSA_HEADS_PER_GROUP, HEAD_DIM, NSA_QCHUNK, SEL_BLOCK
    W = Z * QC
    KT = 2 * SB
    ncp = seq // CMP_STRIDE
    nbs = seq // SEL_BLOCK
    c = pl.program_id(2)
    t0 = c * QC

    @pl.when(c == 0)
    def _():
        step = 4 * QC
        for r in range(seq // step):
            rows = slice(r * step, (r + 1) * step)
            vst_ref[0:dh, rows] = _transpose_bf16(vs_ref[0, 0, rows, :]).astype(BF16)
            vwt_ref[0:dh, rows] = _transpose_bf16(vw_ref[0, 0, rows, :]).astype(BF16)
        ones_row = jnp.where(lax.broadcasted_iota(I32, (VT_ROWS - dh, seq), 0) == 0, 1.0, 0.0).astype(BF16)
        vst_ref[dh:VT_ROWS, :] = ones_row
        vwt_ref[dh:VT_ROWS, :] = ones_row

    q_t = _transpose_bf16(q_ref[0]).astype(BF16)
    qt = jnp.concatenate([q_t[z * dh:(z + 1) * dh] for z in range(Z)], axis=1)
    qt = qt * jnp.asarray(dh ** -0.5, BF16)
    qpos = t0 + (lax.broadcasted_iota(I32, (1, W), 1) & (QC - 1))
    d0 = pl.multiple_of(t0, QC)
    band = QC + WINDOW
    w0 = pl.multiple_of(jnp.maximum(t0 - WINDOW, 0), QC)

    def tile4(row):
        return jnp.concatenate([row] * Z, axis=1)

    s_c = _dot(kc_ref[0, 0], qt)
    s_w = _dot(kw_ref[0, 0, pl.ds(w0, band), :], qt)
    s_d = _dot(ks_ref[0, 0, pl.ds(d0, KT), :], qt)
    s_ref[0] = _dot(ks_ref[0, 0, pl.ds(0, KT), :], qt)

    cmp_end = lax.broadcasted_iota(I32, (ncp, 1), 0) * CMP_STRIDE + (CMP_BLOCK - 1)
    e_c, l_c = _exp_cols(s_c, cmp_end <= qpos)
    p_c = e_c * (1.0 / jnp.maximum(l_c, 1e-30))
    o_c = _dot(vct_ref[0, 0], p_c.astype(BF16))

    p_sum = p_c[:, 0:QC]
    for z in range(1, Z):
        p_sum = p_sum + p_c[:, z * QC:(z + 1) * QC]
    p_hi, p_lo = _split(p_sum)
    imp = _dot(c2st_ref[...], p_hi) + _dot(c2st_ref[...], p_lo)

    jf = lax.broadcasted_iota(I32, (nbs, QC), 0).astype(F32)
    qblk = (qpos[:, :QC] // SEL_BLOCK).astype(F32)
    future = jf > qblk
    forced = (jf == 0.0) | (jf == qblk) | (jf == qblk - 1.0)
    score = jnp.where(future, -1.0, jnp.where(forced, 1e3, imp))
    selm = jnp.zeros((nbs, QC), F32)
    for _ in range(n_sel):
        m = jnp.max(score, axis=0, keepdims=True)
        idx = jnp.min(jnp.where(score == m, jf, float(nbs)), axis=0, keepdims=True)
        hit = jf == idx
        selm = jnp.where(hit, jnp.where(m >= 0.0, 1.0, 0.0), selm)
        score = jnp.where(hit, NEG, score)
    selw = tile4(selm)
    sel_ref[...] = jnp.where(lax.broadcasted_iota(I32, (nbs, W), 0) < 2 * c, selw, 0.0)

    brow = lax.broadcasted_iota(I32, (nbs, 1), 0)
    on_a = jnp.sum(jnp.where(brow == 2 * c, selw, 0.0), axis=0, keepdims=True)
    on_b = jnp.sum(jnp.where(brow == 2 * c + 1, selw, 0.0), axis=0, keepdims=True)
    krow = lax.broadcasted_iota(I32, (KT, 1), 0)
    ok_d = jnp.where(t0 + krow <= qpos, jnp.where(krow < SB, on_a, on_b), 0.0) > 0.5
    sm = jnp.where(ok_d, s_d, NEG)
    m0 = jnp.max(sm, axis=0, keepdims=True)
    p_d = jnp.exp(sm - m0)
    acc0 = _dot(vst_ref[:, pl.ds(d0, KT)], p_d.astype(BF16))
    p_ref[0] = jnp.zeros((KT, W), BF16)

    dist = qpos - (w0 + lax.broadcasted_iota(I32, (band, 1), 0))
    sm_w = jnp.where(jnp.where(dist >= 0, dist, WINDOW) < WINDOW, s_w, NEG)
    e_w = jnp.exp(sm_w - jnp.max(sm_w, axis=0, keepdims=True))
    acc_w = _dot(vwt_ref[:, pl.ds(w0, band)], e_w.astype(BF16))
    o_w = acc_w[0:dh] * (1.0 / jnp.maximum(acc_w[dh:dh + 1], 1e-30))

    def stage(t, src, dst, carry):
        m_i, acc = carry
        tp = jnp.maximum(t - 1, 0)
        k_next = pl.multiple_of(jnp.minimum(t + 1, c) * KT, KT)
        s_ref[dst] = _dot(ks_ref[0, 0, pl.ds(k_next, KT), :], qt)
        vt = vst_ref[:, pl.ds(pl.multiple_of(tp * KT, KT), KT)]
        pv_a = _dot(vt[:, :SB], p_ref[src, 0:SB, :])
        pv_b = _dot(vt[:, SB:], p_ref[src, SB:KT, :])
        on_a = sel_ref[pl.ds(2 * t, 1), :] > 0.5
        on_b = sel_ref[pl.ds(2 * t + 1, 1), :] > 0.5
        prev_a = sel_ref[pl.ds(2 * tp, 1), :] > 0.5
        prev_b = sel_ref[pl.ds(2 * tp + 1, 1), :] > 0.5
        mx_a = jnp.max(s_ref[src, 0:SB, :], axis=0, keepdims=True)
        mx_b = jnp.max(s_ref[src, SB:KT, :], axis=0, keepdims=True)
        m_new = jnp.maximum(m_i, jnp.maximum(jnp.where(on_a, mx_a, NEG), jnp.where(on_b, mx_b, NEG)))
        p_ref[dst] = jnp.exp(s_ref[src] - m_new).astype(BF16)
        acc_new = jnp.exp(m_i - m_new) * (acc + jnp.where(prev_a, pv_a, 0.0) + jnp.where(prev_b, pv_b, 0.0))
        return m_new, acc_new

    n_pairs = (c + 1) // 2
    _, acc_s = lax.fori_loop(0, n_pairs, lambda u, cr: stage(2 * u + 1, 1, 0, stage(2 * u, 0, 1, cr)), (m0, acc0))
    tl = jnp.maximum(2 * n_pairs - 1, 0)
    vt = vst_ref[:, pl.ds(pl.multiple_of(tl * KT, KT), KT)]
    last_a = sel_ref[pl.ds(2 * tl, 1), :] > 0.5
    last_b = sel_ref[pl.ds(2 * tl + 1, 1), :] > 0.5
    acc_s = (acc_s + jnp.where(last_a, _dot(vt[:, :SB], p_ref[0, 0:SB, :]), 0.0)
             + jnp.where(last_b, _dot(vt[:, SB:], p_ref[0, SB:KT, :]), 0.0))
    o_s = acc_s[0:dh] / jnp.maximum(acc_s[dh:dh + 1], 1e-30)

    gt = gt_ref[0, 0]
    g_c = jnp.concatenate([gt[3 * z:3 * z + 1] for z in range(Z)], axis=1)
    g_s = jnp.concatenate([gt[3 * z + 1:3 * z + 2] for z in range(Z)], axis=1)
    g_w = jnp.concatenate([gt[3 * z + 2:3 * z + 3] for z in range(Z)], axis=1)
    o_t = (g_c * o_c + g_s * o_s + g_w * o_w).astype(BF16)
    o_ref[0] = jnp.concatenate([_transpose_bf16(o_t[:, z * QC:(z + 1) * QC]) for z in range(Z)],
                               axis=1).astype(o_ref.dtype)


def _nsa_attention(q, kc, vct, ks, vs, kw, vw, gates_t, c2st, seq):
    B, G, _, dh = ks.shape
    Z, QC = NSA_HEADS_PER_GROUP, NSA_QCHUNK
    W = Z * QC
    KT = 2 * SEL_BLOCK
    ncp = seq // CMP_STRIDE
    nbs = seq // SEL_BLOCK
    full = lambda shp: pl.BlockSpec((1, 1) + shp, lambda b, g, c: (b, g, 0, 0))
    return pl.pallas_call(
        functools.partial(_nsa_attn_kernel, seq=seq, n_sel=min(SEL_TOPN, nbs)),
        grid=(B, G, seq // QC),
        in_specs=[pl.BlockSpec((1, QC, Z * dh), lambda b, g, c: (b, c, g)),
                  full((ncp, dh)), full((dh, ncp)), full((seq, dh)), full((seq, dh)), full((seq, dh)), full((seq, dh)),
                  pl.BlockSpec((1, 1, gates_t.shape[2], QC), lambda b, g, c: (b, g, 0, c)),
                  pl.BlockSpec((nbs, ncp), lambda b, g, c: (0, 0))],
        out_specs=pl.BlockSpec((1, QC, Z * dh), lambda b, g, c: (b, c, g)),
        out_shape=jax.ShapeDtypeStruct((B, seq, G * Z * dh), BF16),
        scratch_shapes=[pltpu.VMEM((VT_ROWS, seq), BF16), pltpu.VMEM((VT_ROWS, seq), BF16),
                        pltpu.VMEM((nbs, W), F32), pltpu.VMEM((2, KT, W), F32), pltpu.VMEM((2, KT, W), BF16)],
        compiler_params=_cparams("parallel", "parallel", "arbitrary"), name="nsa_attn",
    )(q, kc, vct, ks, vs, kw, vw, gates_t, c2st)


def _cmul(ar, ai, br, bi):
    return ar * br - ai * bi, ar * bi + ai * br


def _s5_ops_kernel(lre_ref, lim_ref, lstep_ref, bret_ref, bimt_ref, cre_ref, cim_ref,
                   mt_ref, pm_ref, qt_ref, al_ref):
    L, C, P = S5_CHUNK, S5_GROUP_CH, S5_STATE
    lr = lre_ref[0]
    li = lim_ref[0]
    step = jnp.exp(lstep_ref[0])
    mag = jnp.exp(lr * step)
    ar = mag * jnp.cos(li * step)
    ai = mag * jnp.sin(li * step)
    den = lr * lr + li * li
    fr = ((ar - 1.0) * lr + ai * li) / den
    fi = (ai * lr - (ar - 1.0) * li) / den
    bbr, bbi = _cmul(fr, fi, bret_ref[0], bimt_ref[0])

    up_r, up_i = jnp.ones((1, 1, P), F32), jnp.zeros((1, 1, P), F32)
    dn_r, dn_i = up_r, up_i
    sr, si = ar.reshape(1, 1, P), ai.reshape(1, 1, P)
    a1r, a1i = sr, si
    n = 1
    while n < L:
        tr, ti = _cmul(up_r, up_i, sr, si)
        up_r, up_i = jnp.concatenate([up_r, tr], 0), jnp.concatenate([up_i, ti], 0)
        tr, ti = _cmul(dn_r, dn_i, sr, si)
        dn_r, dn_i = jnp.concatenate([tr, dn_r], 0), jnp.concatenate([ti, dn_i], 0)
        sr, si = _cmul(sr, si, sr, si)
        n *= 2
    al_ref[0] = jnp.concatenate([sr.reshape(1, P), si.reshape(1, P)], -1)

    cr = cre_ref[0][None]
    ci = cim_ref[0][None]
    nr, ni = _cmul(up_r, up_i, a1r, a1i)
    wr, wi = _cmul(cr, ci, nr, ni)
    qt_ref[0] = jnp.concatenate([wr.reshape(L * C, P), -wi.reshape(L * C, P)], -1)
    er, ei = _cmul(dn_r, dn_i, bbr[None], bbi[None])
    pm_ref[0] = jnp.concatenate([er.reshape(L * C, P), ei.reshape(L * C, P)], -1).astype(pm_ref.dtype)
    w0r, w0i = _cmul(cr, ci, up_r, up_i)
    kt = _dot3_nt(bbr, w0r.reshape(L * C, P)) - _dot3_nt(bbi, w0i.reshape(L * C, P))
    lane = lax.broadcasted_iota(I32, kt.shape, 1)
    mt_ref[0, 0:C, :] = kt.astype(mt_ref.dtype)
    for s in range(1, L):
        shifted = jnp.where(lane >= s * C, pltpu.roll(kt, s * C, 1), 0.0)
        mt_ref[0, s * C:(s + 1) * C, :] = shifted.astype(mt_ref.dtype)


def _s5_operators(lre, lim, lstep, bret, bimt, cre, cim):
    G = lre.shape[0]
    L, C, P = S5_CHUNK, S5_GROUP_CH, S5_STATE
    vec = pl.BlockSpec((1, 1, P), lambda g: (g, 0, 0))
    mat = pl.BlockSpec((1, C, P), lambda g: (g, 0, 0))
    return pl.pallas_call(
        _s5_ops_kernel,
        grid=(G,),
        in_specs=[vec, vec, pl.BlockSpec((1, 1, 1), lambda g: (g, 0, 0)), mat, mat, mat, mat],
        out_specs=[pl.BlockSpec((1, L * C, L * C), lambda g: (g, 0, 0)),
                   pl.BlockSpec((1, L * C, 2 * P), lambda g: (g, 0, 0)),
                   pl.BlockSpec((1, L * C, 2 * P), lambda g: (g, 0, 0)),
                   pl.BlockSpec((1, 1, 2 * P), lambda g: (g, 0, 0))],
        out_shape=[jax.ShapeDtypeStruct((G, L * C, L * C), BF16),
                   jax.ShapeDtypeStruct((G, L * C, 2 * P), BF16),
                   jax.ShapeDtypeStruct((G, L * C, 2 * P), F32),
                   jax.ShapeDtypeStruct((G, 1, 2 * P), F32)],
        compiler_params=_cparams("parallel"), name="s5_operators",
    )(lre, lim, lstep, bret, bimt, cre, cim)


def _s5_scan_kernel(u_ref, mt_ref, pm_ref, qt_ref, al_ref, d_ref, y_ref, xin_ref, *, batch):
    P = S5_STATE
    rows = u_ref.shape[1]
    u = u_ref[0]
    ub = u.astype(BF16)
    y_ref[0] = _dot(ub, mt_ref[0]) + d_ref[0] * u
    xloc = _dot(ub, pm_ref[0])
    al = al_ref[0]
    lane = lax.broadcasted_iota(I32, (1, 2 * P), 1)
    mul_same = jnp.where(lane < P, al, pltpu.roll(al, P, 1))
    mul_swap = jnp.where(lane < P, -pltpu.roll(al, P, 1), al)
    xin_ref[...] = xloc

    def carry_step(k, state):
        r0 = pl.multiple_of(k * batch, batch)
        loc = xin_ref[pl.ds(r0, batch), :]
        xin_ref[pl.ds(r0, batch), :] = state
        return state * mul_same + pltpu.roll(state, P, 1) * mul_swap + loc

    lax.fori_loop(0, rows // batch, carry_step, jnp.zeros((batch, 2 * P), F32))
    y_ref[0] += _dot3_nt(xin_ref[...], qt_ref[0])


def _s5_scan(u, mt, pm, qt, al, d, batch):
    G, rows, width = u.shape
    P2 = 2 * S5_STATE
    per_g = lambda shp: pl.BlockSpec((1,) + shp, lambda g: (g, 0, 0))
    return pl.pallas_call(
        functools.partial(_s5_scan_kernel, batch=batch),
        grid=(G,),
        in_specs=[per_g((rows, width)), per_g((width, width)), per_g((width, P2)), per_g((width, P2)),
                  per_g((1, P2)), per_g((1, width))],
        out_specs=per_g((rows, width)),
        out_shape=jax.ShapeDtypeStruct((G, rows, width), F32),
        scratch_shapes=[pltpu.VMEM((rows, P2), F32)],
        compiler_params=_cparams("parallel"), name="s5_scan",
    )(u, mt, pm, qt, al, d)


def _even_out_kernel(x_ref, nsa_ref, y_ref, gw_ref, gb_ref, wa_ref, wb_ref, lg_ref, lb_ref, o_ref, *, alpha):
    g = _gelu_tanh(y_ref[...])
    s5 = g * _sigmoid(_dot(g.astype(BF16), gw_ref[...]) + gb_ref[...])
    mix = _dot(nsa_ref[...], wa_ref[...]) + _dot(s5.astype(BF16), wb_ref[...])
    o_ref[...] = _layer_norm(alpha * x_ref[...] + mix, lg_ref[...], lb_ref[...])


def _even_out(x2d, o_nsa, y_s5, glu_w, glu_b, wa, wb, ln_g, ln_b, alpha):
    T, D = x2d.shape
    tm = min(ROW_TILE, T)
    row = lambda w: pl.BlockSpec((tm, w), lambda i: (i, 0))
    whole = lambda a: pl.BlockSpec(a.shape, lambda i: (0, 0))
    return pl.pallas_call(
        functools.partial(_even_out_kernel, alpha=alpha),
        grid=(T // tm,),
        in_specs=[row(D), row(o_nsa.shape[1]), row(y_s5.shape[1]), whole(glu_w), whole(glu_b), whole(wa), whole(wb),
                  whole(ln_g), whole(ln_b)],
        out_specs=row(D), out_shape=jax.ShapeDtypeStruct((T, D), F32),
        compiler_params=_cparams("parallel"), name="even_out",
    )(x2d, o_nsa, y_s5, glu_w, glu_b, wa, wb, ln_g, ln_b)


def _odd_out_kernel(x_ref, at_ref, w_ref, lg_ref, lb_ref, o_ref, *, alpha):
    attn = _transpose_bf16(at_ref[0]).astype(BF16)
    mix = _dot(attn, w_ref[...])
    o_ref[...] = _layer_norm(alpha * x_ref[...] + mix, lg_ref[...], lb_ref[...])


def _odd_out(x2d, attn_t, w, ln_g, ln_b, alpha):
    T, D = x2d.shape
    _, width, seq = attn_t.shape
    tm = min(ROW_TILE, seq)
    tiles = seq // tm
    row = lambda wd: pl.BlockSpec((tm, wd), lambda i: (i, 0))
    whole = lambda a: pl.BlockSpec(a.shape, lambda i: (0, 0))
    return pl.pallas_call(
        functools.partial(_odd_out_kernel, alpha=alpha),
        grid=(T // tm,),
        in_specs=[row(D), pl.BlockSpec((1, width, tm), lambda i: (i // tiles, 0, i % tiles)),
                  whole(w), whole(ln_g), whole(ln_b)],
        out_specs=row(D), out_shape=jax.ShapeDtypeStruct((T, D), F32),
        compiler_params=_cparams("parallel"), name="odd_out",
    )(x2d, attn_t, w, ln_g, ln_b)


def _eye(n):
    return (lax.broadcasted_iota(I32, (n, n), 0) == lax.broadcasted_iota(I32, (n, n), 1)).astype(BF16)


def _transpose_bf16(a):
    return _dot_nt(_eye(a.shape[1]), a)


def _moba_kernel(q_ref, k_ref, v_ref, o_ref, kmean_ref, vt_ref, sel_ref, s_ref, p_ref, *, seq, top):
    BLK, dh, NH = MOBA_BLOCK, HEAD_DIM, MOBA_HEADS_PER_STEP
    nb = seq // BLK
    i = pl.program_id(2)

    @pl.when(i == 0)
    def _():
        kmean_ref[...] = jnp.mean(k_ref[0].astype(F32).reshape(nb, BLK, NH * dh), axis=1)
        for r in range(nb):
            cols = slice(r * BLK, (r + 1) * BLK)
            v_t = _transpose_bf16(v_ref[0, cols, :]).astype(BF16)
            for h in range(NH):
                vt_ref[h, 0:dh, cols] = v_t[h * dh:(h + 1) * dh]
        ones_row = jnp.where(lax.broadcasted_iota(I32, (VT_ROWS - dh, seq), 0) == 0, 1.0, 0.0).astype(BF16)
        for h in range(NH):
            vt_ref[h, dh:VT_ROWS, :] = ones_row

    jf = lax.broadcasted_iota(I32, (nb, BLK), 0).astype(F32)
    i_f = i.astype(F32)
    causal = lax.broadcasted_iota(I32, (BLK, BLK), 0) <= lax.broadcasted_iota(I32, (BLK, BLK), 1)

    heads = tuple(slice(h * dh, (h + 1) * dh) for h in range(NH))
    qs = [q_ref[0, :, hs] * jnp.asarray(dh ** -0.5, BF16) for hs in heads]

    def scores(j, h):
        c0 = pl.multiple_of(j * BLK, BLK)
        return _dot_nt(k_ref[0, pl.ds(c0, BLK), heads[h]], qs[h])

    def values(j, h, p):
        c0 = pl.multiple_of(j * BLK, BLK)
        return _dot(vt_ref[h, :, pl.ds(c0, BLK)], p)

    gates = []
    for h in range(NH):
        km_hi, km_lo = _split(kmean_ref[:, heads[h]])
        gates.append(_dot_nt(km_hi, qs[h]) + _dot_nt(km_lo, qs[h]))
    s_own = [scores(i, h) for h in range(NH)]
    s_first = [scores(0, h) for h in range(NH)]

    start = []
    for h in range(NH):
        gate = jnp.where(jf < i_f, gates[h], NEG)
        selm = jnp.zeros((nb, BLK), F32)
        for _ in range(top):
            m = jnp.max(gate, axis=0, keepdims=True)
            idx = jnp.min(jnp.where(gate == m, jf, float(nb)), axis=0, keepdims=True)
            hit = jf == idx
            selm = jnp.where(hit, jnp.where(m > 0.5 * NEG, 1.0, 0.0), selm)
            gate = jnp.where(hit, 2.0 * NEG, gate)
        sel_ref[h] = jnp.concatenate([selm, jnp.ones((SUBLANES, BLK), F32)], axis=0)
        sm = jnp.where(causal, s_own[h], NEG)
        m0 = jnp.max(sm, axis=0, keepdims=True)
        p = jnp.exp(sm - m0)
        s_ref[0, h] = s_first[h]
        p_ref[0, h] = p.astype(BF16)
        start += [m0, jnp.zeros((VT_ROWS, BLK), F32)]

    def prev_of(t):
        return jnp.where(t == 0, i, t - 1), jnp.where(t == 0, nb, t - 1)

    def stage(t, src, dst, carry):
        blk_prev, row_prev = prev_of(t)
        nxt = jnp.minimum(t + 1, nb - 1)
        for h in range(NH):
            s_ref[dst, h] = scores(nxt, h)
        pv = [values(blk_prev, h, p_ref[src, h]) for h in range(NH)]
        out = []
        for h in range(NH):
            m_i, acc = carry[2 * h:2 * h + 2]
            on = sel_ref[h, pl.ds(t, 1), :] > 0.5
            on_prev = sel_ref[h, pl.ds(row_prev, 1), :] > 0.5
            m_new = jnp.where(on, jnp.maximum(m_i, jnp.max(s_ref[src, h], axis=0, keepdims=True)), m_i)
            p_ref[dst, h] = jnp.exp(s_ref[src, h] - m_new).astype(BF16)
            out += [m_new, jnp.exp(m_i - m_new) * (acc + jnp.where(on_prev, pv[h], 0.0))]
        return tuple(out)

    n_pairs = (i + 1) // 2
    fin = lax.fori_loop(0, n_pairs, lambda u, c: stage(2 * u + 1, 1, 0, stage(2 * u, 0, 1, c)), tuple(start))
    blk_prev, row_prev = prev_of(2 * n_pairs)
    outs = []
    for h in range(NH):
        on_prev = sel_ref[h, pl.ds(row_prev, 1), :] > 0.5
        acc = fin[2 * h + 1] + jnp.where(on_prev, values(blk_prev, h, p_ref[0, h]), 0.0)
        outs.append(acc[0:dh] / acc[dh:dh + 1])
    o_ref[0] = jnp.concatenate(outs, axis=0).astype(o_ref.dtype)


def _moba_attention(q, k, v, seq):
    B, _, width = q.shape
    BLK, NH = MOBA_BLOCK, MOBA_HEADS_PER_STEP
    hw = NH * HEAD_DIM
    nb = seq // BLK
    top = min(MOBA_TOPK, max(nb - 1, 1))
    return pl.pallas_call(
        functools.partial(_moba_kernel, seq=seq, top=top),
        grid=(B, width // hw, nb),
        in_specs=[pl.BlockSpec((1, BLK, hw), lambda b, h, i: (b, i, h)),
                  pl.BlockSpec((1, seq, hw), lambda b, h, i: (b, 0, h)),
                  pl.BlockSpec((1, seq, hw), lambda b, h, i: (b, 0, h))],
        out_specs=pl.BlockSpec((1, hw, BLK), lambda b, h, i: (b, h, i)),
        out_shape=jax.ShapeDtypeStruct((B, width, seq), BF16),
        scratch_shapes=[pltpu.VMEM((nb, hw), F32), pltpu.VMEM((NH, VT_ROWS, seq), BF16),
                        pltpu.VMEM((NH, nb + SUBLANES, BLK), F32),
                        pltpu.VMEM((2, NH, BLK, BLK), F32), pltpu.VMEM((2, NH, BLK, BLK), BF16)],
        compiler_params=_cparams("parallel", "parallel", "arbitrary"), name="moba_attn",
    )(q, k, v)


def _route(logits):
    n_fine = MOE_GROUPS * MOE_EXPERTS_PER_GROUP
    lane = lax.broadcasted_iota(I32, logits.shape, 1)
    lf = lane.astype(F32)
    is_coarse = (lane >= n_fine) & (lane < n_fine + MOE_GROUPS)
    pc = _masked_softmax(logits, is_coarse)
    gv = jnp.max(pc, axis=-1, keepdims=True)
    gidx = jnp.min(jnp.where(is_coarse & (pc == gv), lf - float(n_fine), float(MOE_GROUPS)), axis=-1, keepdims=True)
    in_group = (lane < n_fine) & ((lane // MOE_EXPERTS_PER_GROUP).astype(F32) == gidx)
    pf = _masked_softmax(logits, in_group)
    cand = jnp.where(in_group, pf, -1.0)
    m1 = jnp.max(cand, axis=-1, keepdims=True)
    i1 = jnp.min(jnp.where(cand == m1, lf, float(LANES)), axis=-1, keepdims=True)
    cand = jnp.where(lf == i1, -1.0, cand)
    m2 = jnp.max(cand, axis=-1, keepdims=True)
    i2 = jnp.min(jnp.where(cand == m2, lf, float(LANES)), axis=-1, keepdims=True)
    tot = m1 + m2
    return jnp.where(lf == i1, gv * (m1 / tot), jnp.where(lf == i2, gv * (m2 / tot), 0.0))


def _moe_kernel(x_ref, wr_ref, br_ref, wg_ref, wu_ref, wd_ref, lg_ref, lb_ref, o_ref, gates_ref, acc_ref, *, alpha):
    E, FF = MOE_EXPERTS_PER_GROUP, MOE_FF
    g = pl.program_id(1)
    x = x_ref[...]

    @pl.when(g == 0)
    def _():
        gates_ref[...] = _route(_dot3(x, wr_ref[...]) + br_ref[...])
        acc_ref[...] = jnp.zeros_like(acc_ref)

    xb = x.astype(BF16)
    hg = _dot(xb, wg_ref[0])
    hu = _dot(xb, wu_ref[0])
    h = hg * _sigmoid(hg) * hu
    gates = gates_ref[...]
    lane = lax.broadcasted_iota(I32, gates.shape, 1)
    parts = []
    for e in range(E):
        ge = jnp.sum(jnp.where(lane == g * E + e, gates, 0.0), axis=-1, keepdims=True)
        parts.append((h[:, e * FF:(e + 1) * FF] * ge).astype(BF16))
    acc_ref[...] += _dot(jnp.concatenate(parts, axis=-1), wd_ref[0])

    @pl.when(g == pl.num_programs(1) - 1)
    def _():
        o_ref[...] = _layer_norm(alpha * x + acc_ref[...], lg_ref[...], lb_ref[...])


def _moe(x2d, wr, br, wg, wu, wd, ln_g, ln_b, alpha):
    T, D = x2d.shape
    tm = min(ROW_TILE, T)
    NG = wg.shape[0]
    whole = lambda a: pl.BlockSpec(a.shape, lambda i, g: (0, 0))
    per_g = lambda a: pl.BlockSpec((1,) + a.shape[1:], lambda i, g: (g, 0, 0))
    return pl.pallas_call(
        functools.partial(_moe_kernel, alpha=alpha),
        grid=(T // tm, NG),
        in_specs=[pl.BlockSpec((tm, D), lambda i, g: (i, 0)), whole(wr), whole(br), per_g(wg), per_g(wu), per_g(wd),
                  whole(ln_g), whole(ln_b)],
        out_specs=pl.BlockSpec((tm, D), lambda i, g: (i, 0)),
        out_shape=jax.ShapeDtypeStruct((T, D), F32),
        scratch_shapes=[pltpu.VMEM((tm, LANES), F32), pltpu.VMEM((tm, D), F32)],
        compiler_params=_cparams("parallel", "arbitrary"), name="moe",
    )(x2d, wr, br, wg, wu, wd, ln_g, ln_b)


def _rope_tables(seq):
    inv = 1.0 / (ROPE_THETA ** (jnp.arange(0, HEAD_DIM, 2, dtype=F32) / HEAD_DIM))
    ang = jnp.arange(seq, dtype=F32)[:, None] * inv[None, :]
    cos, sin = jnp.cos(ang), jnp.sin(ang)
    reps = LANES // HEAD_DIM
    return jnp.tile(jnp.concatenate([cos, cos], -1), (1, reps)), jnp.tile(jnp.concatenate([-sin, sin], -1), (1, reps))


def _selection_constants(seq):
    ncp = seq // CMP_STRIDE
    nbs = seq // SEL_BLOCK
    starts = jnp.arange(ncp) * CMP_STRIDE
    bstart = jnp.arange(nbs) * SEL_BLOCK
    ovl = jnp.clip(jnp.minimum(starts[:, None] + CMP_BLOCK, bstart[None, :] + SEL_BLOCK)
                   - jnp.maximum(starts[:, None], bstart[None, :]), 0, CMP_BLOCK)
    return (ovl.astype(F32) / CMP_BLOCK).astype(BF16).T


def _moe_weights(w_coarse, b_coarse, w_fine, b_fine, w_gate, w_up, w_down):
    D = w_coarse.shape[0]
    NG, E, _, FF = w_gate.shape
    n_fine = NG * E
    wr = jnp.zeros((D, LANES), F32)
    wr = wr.at[:, :n_fine].set(w_fine.transpose(1, 0, 2).reshape(D, n_fine)).at[:, n_fine:n_fine + NG].set(w_coarse)
    br = jnp.zeros((1, LANES), F32)
    br = br.at[0, :n_fine].set(b_fine.reshape(n_fine)).at[0, n_fine:n_fine + NG].set(b_coarse)
    wg = w_gate.transpose(0, 2, 1, 3).reshape(NG, D, E * FF).astype(BF16)
    wu = w_up.transpose(0, 2, 1, 3).reshape(NG, D, E * FF).astype(BF16)
    wd = w_down.reshape(NG, E * FF, D).astype(BF16)
    return wr, br, wg, wu, wd


def _even_mixer(x2d, B, S, cos2, sin2, w_in, pe_k, pe_v, k_w1, k_w2, v_w1, v_w2,
                lam_re, lam_im, log_step, b_re, b_im, c_re, c_im, d_skip, glu_w, glu_b, w_out, ln_g, ln_b, alpha):
    G, Z, dh = NSA_KV_GROUPS, NSA_HEADS_PER_GROUP, HEAD_DIM
    kvw = NSA_KV_WIDTH
    D = x2d.shape[1]
    cuts = [NSA_WIDTH + i * kvw for i in range(7)]
    w_q, w_kc, w_vc, w_ks, w_vs, w_kw, w_vw = [w_in[:, a:b] for a, b in zip([0] + cuts[:-1], cuts)]
    n_gate = 3 * NSA_HEADS
    w_g = jnp.pad(w_in[:, cuts[-1]:cuts[-1] + n_gate], ((0, 0), (0, LANES - n_gate)))
    w_u = w_in[:, cuts[-1] + n_gate:]
    weights = [w_q.astype(BF16), jnp.concatenate([w_kc, w_ks, w_kw], 1).astype(BF16),
               jnp.concatenate([w_vc, w_vs, w_vw], 1).astype(BF16), w_u.astype(BF16), w_g.astype(BF16)]
    q, kk, vv, u, gates = _project(x2d, cos2, sin2, weights, ["rope", "rope", "plain", "plain", "sigmoid"],
                                   [BF16, BF16, BF16, F32, F32], S)

    def heads(t):
        return t.reshape(B, S, G, dh).transpose(0, 2, 1, 3)

    kc, ks, kw = [heads(kk[:, i * kvw:(i + 1) * kvw]) for i in range(3)]
    vc, vs, vw = [heads(vv[:, i * kvw:(i + 1) * kvw]) for i in range(3)]
    gates_t = gates[:, :n_gate].reshape(B, S, G, 3 * Z).transpose(0, 2, 3, 1)

    n_rows = S // CMP_STRIDE
    r = jnp.stack([kc, vc]).reshape(2, B * G, n_rows, CMP_STRIDE * dh)
    pe = jnp.stack([pe_k, pe_v]).reshape(2, 1, CMP_BLOCK * dh)
    w1 = jnp.stack([k_w1, v_w1]).astype(BF16)
    w2 = jnp.stack([k_w2, v_w2]).astype(BF16)
    cmp, cmp_t = _nsa_compress(r, pe, w1, w2, w2.transpose(0, 2, 1))
    kcm = cmp[0].reshape(B, G, n_rows, dh)
    vct = cmp_t[1].reshape(B, G, dh, n_rows)

    o_nsa = _nsa_attention(q.reshape(B, S, NSA_WIDTH), kcm, vct, ks, vs, kw, vw, gates_t, _selection_constants(S), S)

    C, L = S5_GROUP_CH, S5_CHUNK
    SG = u.shape[1] // C
    mt, pm, qt, al = _s5_operators(lam_re[:, None, :], lam_im[:, None, :], log_step[:, None, None],
                                   b_re.transpose(0, 2, 1), b_im.transpose(0, 2, 1), c_re, c_im)
    u_g = u.reshape(B, S // L, L, SG, C).transpose(3, 1, 0, 2, 4).reshape(SG, (S // L) * B, L * C)
    d_t = jnp.tile(d_skip.reshape(SG, 1, C), (1, 1, L))
    y = _s5_scan(u_g, mt, pm, qt, al, d_t, B)
    y = y.reshape(SG, S // L, B, L, C).transpose(2, 1, 3, 0, 4).reshape(B * S, SG * C)

    return _even_out(x2d, o_nsa.reshape(B * S, NSA_WIDTH), y, glu_w.astype(BF16), glu_b[None, :],
                     w_out[:NSA_WIDTH].astype(BF16), w_out[NSA_WIDTH:].astype(BF16), ln_g[None, :], ln_b[None, :], alpha)


def _odd_mixer(x2d, B, S, cos2, sin2, w_in, w_out, ln_g, ln_b, alpha):
    D = x2d.shape[1]
    q, k, v = _project(x2d, cos2, sin2, [w_in[:, i * D:(i + 1) * D].astype(BF16) for i in range(3)],
                       ["rope", "rope", "plain"], [BF16, BF16, BF16], S)
    attn_t = _moba_attention(q.reshape(B, S, D), k.reshape(B, S, D), v.reshape(B, S, D), S)
    return _odd_out(x2d, attn_t, w_out.astype(BF16), ln_g[None, :], ln_b[None, :], alpha)


def kernel(x, ev_w_in, nsa_pe_k, nsa_pe_v, nsa_cmp_k_w1, nsa_cmp_k_w2, nsa_cmp_v_w1, nsa_cmp_v_w2, s5_lambda_re, s5_lambda_im, s5_log_step, s5_b_re, s5_b_im, s5_c_re, s5_c_im, s5_d, s5_glu_w, s5_glu_b, ev_w_out, od_w_in, od_w_out, ln_mix_g, ln_mix_b, ln_ffn_g, ln_ffn_b, moe_w_coarse, moe_b_coarse, moe_w_fine, moe_b_fine, moe_w_gate, moe_w_up, moe_w_down):
    B, S, D = x.shape
    depth = ln_mix_g.shape[0]
    alpha = (2.0 * depth) ** 0.25
    cos2, sin2 = _rope_tables(S)
    h = x.reshape(B * S, D)
    for layer in range(depth):
        if layer % 2 == 0:
            e = layer // 2
            h = _even_mixer(h, B, S, cos2, sin2, ev_w_in[e], nsa_pe_k[e], nsa_pe_v[e], nsa_cmp_k_w1[e], nsa_cmp_k_w2[e],
                            nsa_cmp_v_w1[e], nsa_cmp_v_w2[e], s5_lambda_re[e], s5_lambda_im[e], s5_log_step[e],
                            s5_b_re[e], s5_b_im[e], s5_c_re[e], s5_c_im[e], s5_d[e], s5_glu_w[e], s5_glu_b[e],
                            ev_w_out[e], ln_mix_g[layer], ln_mix_b[layer], alpha)
        else:
            o = layer // 2
            h = _odd_mixer(h, B, S, cos2, sin2, od_w_in[o], od_w_out[o], ln_mix_g[layer], ln_mix_b[layer], alpha)
        wr, br, wg, wu, wd = _moe_weights(moe_w_coarse[layer], moe_b_coarse[layer], moe_w_fine[layer], moe_b_fine[layer],
                                          moe_w_gate[layer], moe_w_up[layer], moe_w_down[layer])
        h = _moe(h, wr, br, wg, wu, wd, ln_ffn_g[layer][None, :], ln_ffn_b[layer][None, :], alpha)
    return h.reshape(B, S, D)
```

```python
import functools
import math

import jax
import jax.numpy as jnp
from jax import lax
from jax.experimental import pallas as pl
from jax.experimental.pallas import tpu as pltpu

F32 = jnp.float32
BF16 = jnp.bfloat16
I32 = jnp.int32

HEAD_DIM = 64
ROPE_THETA = 10000.0
LN_EPS = 1e-5

NSA_KV_GROUPS = 2
NSA_HEADS_PER_GROUP = 4
NSA_HEADS = NSA_KV_GROUPS * NSA_HEADS_PER_GROUP
NSA_WIDTH = NSA_HEADS * HEAD_DIM
NSA_KV_WIDTH = NSA_KV_GROUPS * HEAD_DIM
CMP_BLOCK = 32
CMP_STRIDE = 16
SEL_BLOCK = 64
SEL_TOPN = 8
WINDOW = 256
NSA_QCHUNK = 128

S5_GROUP_CH = 16
S5_STATE = 64
S5_CHUNK = 64

MOBA_BLOCK = 256
MOBA_TOPK = 3
MOBA_HEADS_PER_STEP = 8

MOE_GROUPS = 4
MOE_EXPERTS_PER_GROUP = 8
MOE_FF = 128

LANES = 128
SUBLANES = 8
MXU_COLS = 256
VT_ROWS = HEAD_DIM + 16
ROW_TILE = 512
VMEM_LIMIT = 56 * 1024 * 1024
NEG = -1e30


def _cparams(*sem):
    return pltpu.CompilerParams(dimension_semantics=sem, vmem_limit_bytes=VMEM_LIMIT)


def _dot(a, b):
    return jnp.dot(a, b, preferred_element_type=F32)


def _dot_nt(a, b):
    return lax.dot_general(a, b, (((1,), (1,)), ((), ())), preferred_element_type=F32)


def _split(a):
    hi = a.astype(BF16)
    lo = (a - hi.astype(F32)).astype(BF16)
    return hi, lo


def _dot3(a, b):
    ah, al = _split(a)
    bh, bl = _split(b)
    return _dot(ah, bh) + _dot(ah, bl) + _dot(al, bh)


def _dot3_nt(a, b):
    ah, al = _split(a)
    bh, bl = _split(b)
    return _dot_nt(ah, bh) + _dot_nt(ah, bl) + _dot_nt(al, bh)


def _sigmoid(x):
    return 1.0 / (1.0 + jnp.exp(-x))


def _gelu_tanh(x):
    return 0.5 * x * (1.0 + jnp.tanh(math.sqrt(2.0 / math.pi) * (x + 0.044715 * (x * x * x))))


def _layer_norm(r, g, b):
    mu = jnp.mean(r, axis=-1, keepdims=True)
    d = r - mu
    var = jnp.mean(d * d, axis=-1, keepdims=True)
    return d * lax.rsqrt(var + LN_EPS) * g + b


def _masked_softmax(s, ok):
    sm = jnp.where(ok, s, NEG)
    m = jnp.max(sm, axis=-1, keepdims=True)
    p = jnp.where(ok, jnp.exp(sm - m), 0.0)
    return p / jnp.maximum(jnp.sum(p, axis=-1, keepdims=True), 1e-30)


def _rope_128(a, cos2, sin2):
    lane = lax.broadcasted_iota(I32, a.shape, 1)
    first_half = (lane & (HEAD_DIM - 1)) < (HEAD_DIM // 2)
    partner = jnp.where(first_half, pltpu.roll(a, LANES - HEAD_DIM // 2, 1), pltpu.roll(a, HEAD_DIM // 2, 1))
    return a * cos2 + partner * sin2


def _proj_kernel(x_ref, cos_ref, sin_ref, *refs, kinds):
    n = len(kinds)
    w_refs, o_refs = refs[:n], refs[n:]
    xb = x_ref[...].astype(BF16)
    cos2 = cos_ref[...]
    sin2 = sin_ref[...]
    for kind, w_ref, o_ref in zip(kinds, w_refs, o_refs):
        width = w_ref.shape[1]
        for c0 in range(0, width, MXU_COLS):
            cw = min(MXU_COLS, width - c0)
            acc = _dot(xb, w_ref[:, c0:c0 + cw])
            for l0 in range(0, cw, LANES):
                part = acc[:, l0:l0 + LANES]
                if kind.startswith("rope"):
                    part = _rope_128(part, cos2, sin2)
                elif kind == "sigmoid":
                    part = _sigmoid(part)
                if kind.endswith("_heads"):
                    hd = (c0 + l0) // HEAD_DIM
                    o_ref[hd] = part[:, :HEAD_DIM].astype(o_ref.dtype)
                    o_ref[hd + 1] = part[:, HEAD_DIM:].astype(o_ref.dtype)
                else:
                    o_ref[:, c0 + l0:c0 + l0 + LANES] = part.astype(o_ref.dtype)


def _project(x2d, cos2, sin2, weights, kinds, dtypes, seq):
    T, D = x2d.shape
    tm = min(ROW_TILE, seq)
    pos_tiles = seq // tm
    in_specs = [pl.BlockSpec((tm, D), lambda i: (i, 0)),
                pl.BlockSpec((tm, LANES), lambda i: (i % pos_tiles, 0)),
                pl.BlockSpec((tm, LANES), lambda i: (i % pos_tiles, 0))]
    in_specs += [pl.BlockSpec(w.shape, lambda i: (0, 0)) for w in weights]
    out_specs, out_shape = [], []
    for w, kind, dt in zip(weights, kinds, dtypes):
        if kind.endswith("_heads"):
            n_heads = w.shape[1] // HEAD_DIM
            out_specs.append(pl.BlockSpec((n_heads, tm, HEAD_DIM), lambda i: (0, i, 0)))
            out_shape.append(jax.ShapeDtypeStruct((n_heads, T, HEAD_DIM), dt))
        else:
            out_specs.append(pl.BlockSpec((tm, w.shape[1]), lambda i: (i, 0)))
            out_shape.append(jax.ShapeDtypeStruct((T, w.shape[1]), dt))
    return pl.pallas_call(
        functools.partial(_proj_kernel, kinds=tuple(kinds)),
        grid=(T // tm,), in_specs=in_specs, out_specs=out_specs, out_shape=out_shape,
        compiler_params=_cparams("parallel"), name="proj",
    )(x2d, cos2, sin2, *weights)


def _compress_kernel(r_ref, pe_ref, w1_ref, w2_ref, w2t_ref, o_ref, ot_ref):
    half = CMP_STRIDE * HEAD_DIM
    r = r_ref[0, 0]
    n = r.shape[0]
    top = _dot(r, w1_ref[0, :half, :])
    bot = _dot(r, w1_ref[0, half:, :])
    bias = _dot(pe_ref[0].astype(BF16), w1_ref[0])
    h = top + pltpu.roll(bot, n - 1, 0) + bias
    hb = _gelu_tanh(h).astype(BF16)
    o_ref[0, 0] = _dot(hb, w2_ref[0]).astype(o_ref.dtype)
    ot_ref[0, 0] = _dot_nt(w2t_ref[0], hb).astype(ot_ref.dtype)


def _nsa_compress(r, pe, w1, w2, w2t):
    _, bg, n, width = r.shape
    hid = w1.shape[2]
    return pl.pallas_call(
        _compress_kernel,
        grid=(2, bg),
        in_specs=[pl.BlockSpec((1, 1, n, width), lambda a, i: (a, i, 0, 0)),
                  pl.BlockSpec((1, 1, 2 * width), lambda a, i: (a, 0, 0)),
                  pl.BlockSpec((1, 2 * width, hid), lambda a, i: (a, 0, 0)),
                  pl.BlockSpec((1, hid, HEAD_DIM), lambda a, i: (a, 0, 0)),
                  pl.BlockSpec((1, HEAD_DIM, hid), lambda a, i: (a, 0, 0))],
        out_specs=[pl.BlockSpec((1, 1, n, HEAD_DIM), lambda a, i: (a, i, 0, 0)),
                   pl.BlockSpec((1, 1, HEAD_DIM, n), lambda a, i: (a, i, 0, 0))],
        out_shape=[jax.ShapeDtypeStruct((2, bg, n, HEAD_DIM), BF16),
                   jax.ShapeDtypeStruct((2, bg, HEAD_DIM, n), BF16)],
        compiler_params=_cparams("parallel", "parallel"), name="nsa_compress",
    )(r, pe, w1, w2, w2t)


def _exp_cols(s, ok):
    sm = jnp.where(ok, s, NEG)
    m = jnp.max(sm, axis=0, keepdims=True)
    p = jnp.exp(sm - jnp.where(m > 0.5 * NEG, m, 0.0))
    return p, jnp.sum(p, axis=0, keepdims=True)


def _nsa_attn_kernel(q_ref, kc_ref, vct_ref, ks_ref, vs_ref, kw_ref, vw_ref, gt_ref, c2st_ref, o_ref,
                     vst_ref, vwt_ref, sel_ref, s_ref, p_ref, *, seq, n_sel):
    Z, dh, QC, SB = NSA_HEADS_PER_GROUP, HEAD_DIM, NSA_QCHUNK, SEL_BLOCK
    W = Z * QC
    KT = 2 * SB
    ncp = seq // CMP_STRIDE
    nbs = seq // SEL_BLOCK
    c = pl.program_id(2)
    t0 = c * QC

    @pl.when(c == 0)
    def _():
        step = 4 * QC
        for r in range(seq // step):
            rows = slice(r * step, (r + 1) * step)
            vst_ref[0:dh, rows] = _transpose_bf16(vs_ref[0, 0, rows, :]).astype(BF16)
            vwt_ref[0:dh, rows] = _transpose_bf16(vw_ref[0, 0, rows, :]).astype(BF16)
        ones_row = jnp.where(lax.broadcasted_iota(I32, (VT_ROWS - dh, seq), 0) == 0, 1.0, 0.0).astype(BF16)
        vst_ref[dh:VT_ROWS, :] = ones_row
        vwt_ref[dh:VT_ROWS, :] = ones_row

    q_t = _transpose_bf16(q_ref[0]).astype(BF16)
    qt = jnp.concatenate([q_t[z * dh:(z + 1) * dh] for z in range(Z)], axis=1)
    qt = qt * jnp.asarray(dh ** -0.5, BF16)
    qpos = t0 + (lax.broadcasted_iota(I32, (1, W), 1) & (QC - 1))
    d0 = pl.multiple_of(t0, QC)
    band = QC + WINDOW
    w0 = pl.multiple_of(jnp.maximum(t0 - WINDOW, 0), QC)

    def tile4(row):
        return jnp.concatenate([row] * Z, axis=1)

    s_c = _dot(kc_ref[0, 0], qt)
    s_w = _dot(kw_ref[0, 0, pl.ds(w0, band), :], qt)
    s_d = _dot(ks_ref[0, 0, pl.ds(d0, KT), :], qt)
    s_ref[0] = _dot(ks_ref[0, 0, pl.ds(0, KT), :], qt)

    cmp_end = lax.broadcasted_iota(I32, (ncp, 1), 0) * CMP_STRIDE + (CMP_BLOCK - 1)
    e_c, l_c = _exp_cols(s_c, cmp_end <= qpos)
    p_c = e_c * (1.0 / jnp.maximum(l_c, 1e-30))
    o_c = _dot(vct_ref[0, 0], p_c.astype(BF16))

    p_sum = p_c[:, 0:QC]
    for z in range(1, Z):
        p_sum = p_sum + p_c[:, z * QC:(z + 1) * QC]
    p_hi, p_lo = _split(p_sum)
    imp = _dot(c2st_ref[...], p_hi) + _dot(c2st_ref[...], p_lo)

    jf = lax.broadcasted_iota(I32, (nbs, QC), 0).astype(F32)
    qblk = (qpos[:, :QC] // SEL_BLOCK).astype(F32)
    future = jf > qblk
    forced = (jf == 0.0) | (jf == qblk) | (jf == qblk - 1.0)
    score = jnp.where(future, -1.0, jnp.where(forced, 1e3, imp))
    selm = jnp.zeros((nbs, QC), F32)
    for _ in range(n_sel):
        m = jnp.max(score, axis=0, keepdims=True)
        idx = jnp.min(jnp.where(score == m, jf, float(nbs)), axis=0, keepdims=True)
        hit = jf == idx
        selm = jnp.where(hit, jnp.where(m >= 0.0, 1.0, 0.0), selm)
        score = jnp.where(hit, NEG, score)
    selw = tile4(selm)
    sel_ref[...] = jnp.where(lax.broadcasted_iota(I32, (nbs, W), 0) < 2 * c, selw, 0.0)

    brow = lax.broadcasted_iota(I32, (nbs, 1), 0)
    on_a = jnp.sum(jnp.where(brow == 2 * c, selw, 0.0), axis=0, keepdims=True)
    on_b = jnp.sum(jnp.where(brow == 2 * c + 1, selw, 0.0), axis=0, keepdims=True)
    krow = lax.broadcasted_iota(I32, (KT, 1), 0)
    ok_d = jnp.where(t0 + krow <= qpos, jnp.where(krow < SB, on_a, on_b), 0.0) > 0.5
    sm = jnp.where(ok_d, s_d, NEG)
    m0 = jnp.max(sm, axis=0, keepdims=True)
    p_d = jnp.exp(sm - m0)
    acc0 = _dot(vst_ref[:, pl.ds(d0, KT)], p_d.astype(BF16))
    p_ref[0] = jnp.zeros((KT, W), BF16)

    dist = qpos - (w0 + lax.broadcasted_iota(I32, (band, 1), 0))
    sm_w = jnp.where(jnp.where(dist >= 0, dist, WINDOW) < WINDOW, s_w, NEG)
    e_w = jnp.exp(sm_w - jnp.max(sm_w, axis=0, keepdims=True))
    acc_w = _dot(vwt_ref[:, pl.ds(w0, band)], e_w.astype(BF16))
    o_w = acc_w[0:dh] * (1.0 / jnp.maximum(acc_w[dh:dh + 1], 1e-30))

    def stage(t, src, dst, carry):
        m_i, acc = carry
        tp = jnp.maximum(t - 1, 0)
        k_next = pl.multiple_of(jnp.minimum(t + 1, c) * KT, KT)
        s_ref[dst] = _dot(ks_ref[0, 0, pl.ds(k_next, KT), :], qt)
        vt = vst_ref[:, pl.ds(pl.multiple_of(tp * KT, KT), KT)]
        pv_a = _dot(vt[:, :SB], p_ref[src, 0:SB, :])
        pv_b = _dot(vt[:, SB:], p_ref[src, SB:KT, :])
        on_a = sel_ref[pl.ds(2 * t, 1), :] > 0.5
        on_b = sel_ref[pl.ds(2 * t + 1, 1), :] > 0.5
        prev_a = sel_ref[pl.ds(2 * tp, 1), :] > 0.5
        prev_b = sel_ref[pl.ds(2 * tp + 1, 1), :] > 0.5
        mx_a = jnp.max(s_ref[src, 0:SB, :], axis=0, keepdims=True)
        mx_b = jnp.max(s_ref[src, SB:KT, :], axis=0, keepdims=True)
        m_new = jnp.maximum(m_i, jnp.maximum(jnp.where(on_a, mx_a, NEG), jnp.where(on_b, mx_b, NEG)))
        p_ref[dst] = jnp.exp(s_ref[src] - m_new).astype(BF16)
        acc_new = jnp.exp(m_i - m_new) * (acc + jnp.where(prev_a, pv_a, 0.0) + jnp.where(prev_b, pv_b, 0.0))
        return m_new, acc_new

    n_pairs = (c + 1) // 2
    _, acc_s = lax.fori_loop(0, n_pairs, lambda u, cr: stage(2 * u + 1, 1, 0, stage(2 * u, 0, 1, cr)), (m0, acc0))
    tl = jnp.maximum(2 * n_pairs - 1, 0)
    vt = vst_ref[:, pl.ds(pl.multiple_of(tl * KT, KT), KT)]
    last_a = sel_ref[pl.ds(2 * tl, 1), :] > 0.5
    last_b = sel_ref[pl.ds(2 * tl + 1, 1), :] > 0.5
    acc_s = (acc_s + jnp.where(last_a, _dot(vt[:, :SB], p_ref[0, 0:SB, :]), 0.0)
             + jnp.where(last_b, _dot(vt[:, SB:], p_ref[0, SB:KT, :]), 0.0))
    o_s = acc_s[0:dh] / jnp.maximum(acc_s[dh:dh + 1], 1e-30)

    gt = gt_ref[0, 0]
    g_c = jnp.concatenate([gt[3 * z:3 * z + 1] for z in range(Z)], axis=1)
    g_s = jnp.concatenate([gt[3 * z + 1:3 * z + 2] for z in range(Z)], axis=1)
    g_w = jnp.concatenate([gt[3 * z + 2:3 * z + 3] for z in range(Z)], axis=1)
    o_t = (g_c * o_c + g_s * o_s + g_w * o_w).astype(BF16)
    o_ref[0] = jnp.concatenate([_transpose_bf16(o_t[:, z * QC:(z + 1) * QC]) for z in range(Z)],
                               axis=1).astype(o_ref.dtype)


def _nsa_attention(q, kc, vct, ks, vs, kw, vw, gates_t, c2st, seq):
    G, B, _, dh = ks.shape
    Z, QC = NSA_HEADS_PER_GROUP, NSA_QCHUNK
    W = Z * QC
    KT = 2 * SEL_BLOCK
    ncp = seq // CMP_STRIDE
    nbs = seq // SEL_BLOCK
    full = lambda shp: pl.BlockSpec((1, 1) + shp, lambda b, g, c: (g, b, 0, 0))
    return pl.pallas_call(
        functools.partial(_nsa_attn_kernel, seq=seq, n_sel=min(SEL_TOPN, nbs)),
        grid=(B, G, seq // QC),
        in_specs=[pl.BlockSpec((1, QC, Z * dh), lambda b, g, c: (b, c, g)),
                  full((ncp, dh)), full((dh, ncp)), full((seq, dh)), full((seq, dh)), full((seq, dh)), full((seq, dh)),
                  pl.BlockSpec((1, 1, gates_t.shape[2], QC), lambda b, g, c: (b, g, 0, c)),
                  pl.BlockSpec((nbs, ncp), lambda b, g, c: (0, 0))],
        out_specs=pl.BlockSpec((1, QC, Z * dh), lambda b, g, c: (b, c, g)),
        out_shape=jax.ShapeDtypeStruct((B, seq, G * Z * dh), BF16),
        scratch_shapes=[pltpu.VMEM((VT_ROWS, seq), BF16), pltpu.VMEM((VT_ROWS, seq), BF16),
                        pltpu.VMEM((nbs, W), F32), pltpu.VMEM((2, KT, W), F32), pltpu.VMEM((2, KT, W), BF16)],
        compiler_params=_cparams("parallel", "parallel", "arbitrary"), name="nsa_attn",
    )(q, kc, vct, ks, vs, kw, vw, gates_t, c2st)


def _cmul(ar, ai, br, bi):
    return ar * br - ai * bi, ar * bi + ai * br


def _s5_ops_kernel(lre_ref, lim_ref, lstep_ref, bret_ref, bimt_ref, cre_ref, cim_ref,
                   mt_ref, pm_ref, qt_ref, al_ref):
    L, C, P = S5_CHUNK, S5_GROUP_CH, S5_STATE
    lr = lre_ref[0]
    li = lim_ref[0]
    step = jnp.exp(lstep_ref[0])
    mag = jnp.exp(lr * step)
    ar = mag * jnp.cos(li * step)
    ai = mag * jnp.sin(li * step)
    den = lr * lr + li * li
    fr = ((ar - 1.0) * lr + ai * li) / den
    fi = (ai * lr - (ar - 1.0) * li) / den
    bbr, bbi = _cmul(fr, fi, bret_ref[0], bimt_ref[0])

    up_r, up_i = jnp.ones((1, 1, P), F32), jnp.zeros((1, 1, P), F32)
    dn_r, dn_i = up_r, up_i
    sr, si = ar.reshape(1, 1, P), ai.reshape(1, 1, P)
    a1r, a1i = sr, si
    n = 1
    while n < L:
        tr, ti = _cmul(up_r, up_i, sr, si)
        up_r, up_i = jnp.concatenate([up_r, tr], 0), jnp.concatenate([up_i, ti], 0)
        tr, ti = _cmul(dn_r, dn_i, sr, si)
        dn_r, dn_i = jnp.concatenate([tr, dn_r], 0), jnp.concatenate([ti, dn_i], 0)
        sr, si = _cmul(sr, si, sr, si)
        n *= 2
    al_ref[0] = jnp.concatenate([sr.reshape(1, P), si.reshape(1, P)], -1)

    cr = cre_ref[0][None]
    ci = cim_ref[0][None]
    nr, ni = _cmul(up_r, up_i, a1r, a1i)
    wr, wi = _cmul(cr, ci, nr, ni)
    qt_ref[0] = jnp.concatenate([wr.reshape(L * C, P), -wi.reshape(L * C, P)], -1)
    er, ei = _cmul(dn_r, dn_i, bbr[None], bbi[None])
    pm_ref[0] = jnp.concatenate([er.reshape(L * C, P), ei.reshape(L * C, P)], -1).astype(pm_ref.dtype)
    w0r, w0i = _cmul(cr, ci, up_r, up_i)
    kt = _dot3_nt(bbr, w0r.reshape(L * C, P)) - _dot3_nt(bbi, w0i.reshape(L * C, P))
    lane = lax.broadcasted_iota(I32, kt.shape, 1)
    mt_ref[0, 0:C, :] = kt.astype(mt_ref.dtype)
    for s in range(1, L):
        shifted = jnp.where(lane >= s * C, pltpu.roll(kt, s * C, 1), 0.0)
        mt_ref[0, s * C:(s + 1) * C, :] = shifted.astype(mt_ref.dtype)


def _s5_operators(lre, lim, lstep, bret, bimt, cre, cim):
    G = lre.shape[0]
    L, C, P = S5_CHUNK, S5_GROUP_CH, S5_STATE
    vec = pl.BlockSpec((1, 1, P), lambda g: (g, 0, 0))
    mat = pl.BlockSpec((1, C, P), lambda g: (g, 0, 0))
    return pl.pallas_call(
        _s5_ops_kernel,
        grid=(G,),
        in_specs=[vec, vec, pl.BlockSpec((1, 1, 1), lambda g: (g, 0, 0)), mat, mat, mat, mat],
        out_specs=[pl.BlockSpec((1, L * C, L * C), lambda g: (g, 0, 0)),
                   pl.BlockSpec((1, L * C, 2 * P), lambda g: (g, 0, 0)),
                   pl.BlockSpec((1, L * C, 2 * P), lambda g: (g, 0, 0)),
                   pl.BlockSpec((1, 1, 2 * P), lambda g: (g, 0, 0))],
        out_shape=[jax.ShapeDtypeStruct((G, L * C, L * C), BF16),
                   jax.ShapeDtypeStruct((G, L * C, 2 * P), BF16),
                   jax.ShapeDtypeStruct((G, L * C, 2 * P), F32),
                   jax.ShapeDtypeStruct((G, 1, 2 * P), F32)],
        compiler_params=_cparams("parallel"), name="s5_operators",
    )(lre, lim, lstep, bret, bimt, cre, cim)


def _s5_scan_kernel(u_ref, mt_ref, pm_ref, qt_ref, al_ref, d_ref, y_ref, xin_ref, *, batch):
    P = S5_STATE
    rows = u_ref.shape[1]
    u = u_ref[0]
    ub = u.astype(BF16)
    y_ref[0] = _dot(ub, mt_ref[0]) + d_ref[0] * u
    xloc = _dot(ub, pm_ref[0])
    al = al_ref[0]
    lane = lax.broadcasted_iota(I32, (1, 2 * P), 1)
    mul_same = jnp.where(lane < P, al, pltpu.roll(al, P, 1))
    mul_swap = jnp.where(lane < P, -pltpu.roll(al, P, 1), al)
    xin_ref[...] = xloc

    def carry_step(k, state):
        r0 = pl.multiple_of(k * batch, batch)
        loc = xin_ref[pl.ds(r0, batch), :]
        xin_ref[pl.ds(r0, batch), :] = state
        return state * mul_same + pltpu.roll(state, P, 1) * mul_swap + loc

    lax.fori_loop(0, rows // batch, carry_step, jnp.zeros((batch, 2 * P), F32))
    y_ref[0] += _dot3_nt(xin_ref[...], qt_ref[0])


def _s5_scan(u, mt, pm, qt, al, d, batch):
    G, rows, width = u.shape
    P2 = 2 * S5_STATE
    per_g = lambda shp: pl.BlockSpec((1,) + shp, lambda g: (g, 0, 0))
    return pl.pallas_call(
        functools.partial(_s5_scan_kernel, batch=batch),
        grid=(G,),
        in_specs=[per_g((rows, width)), per_g((width, width)), per_g((width, P2)), per_g((width, P2)),
                  per_g((1, P2)), per_g((1, width))],
        out_specs=per_g((rows, width)),
        out_shape=jax.ShapeDtypeStruct((G, rows, width), F32),
        scratch_shapes=[pltpu.VMEM((rows, P2), F32)],
        compiler_params=_cparams("parallel"), name="s5_scan",
    )(u, mt, pm, qt, al, d)


def _even_out_kernel(x_ref, nsa_ref, y_ref, gw_ref, gb_ref, wa_ref, wb_ref, lg_ref, lb_ref, o_ref, *, alpha):
    g = _gelu_tanh(y_ref[...])
    s5 = g * _sigmoid(_dot(g.astype(BF16), gw_ref[...]) + gb_ref[...])
    mix = _dot(nsa_ref[...], wa_ref[...]) + _dot(s5.astype(BF16), wb_ref[...])
    o_ref[...] = _layer_norm(alpha * x_ref[...] + mix, lg_ref[...], lb_ref[...])


def _even_out(x2d, o_nsa, y_s5, glu_w, glu_b, wa, wb, ln_g, ln_b, alpha):
    T, D = x2d.shape
    tm = min(ROW_TILE, T)
    row = lambda w: pl.BlockSpec((tm, w), lambda i: (i, 0))
    whole = lambda a: pl.BlockSpec(a.shape, lambda i: (0, 0))
    return pl.pallas_call(
        functools.partial(_even_out_kernel, alpha=alpha),
        grid=(T // tm,),
        in_specs=[row(D), row(o_nsa.shape[1]), row(y_s5.shape[1]), whole(glu_w), whole(glu_b), whole(wa), whole(wb),
                  whole(ln_g), whole(ln_b)],
        out_specs=row(D), out_shape=jax.ShapeDtypeStruct((T, D), F32),
        compiler_params=_cparams("parallel"), name="even_out",
    )(x2d, o_nsa, y_s5, glu_w, glu_b, wa, wb, ln_g, ln_b)


def _odd_out_kernel(x_ref, at_ref, w_ref, lg_ref, lb_ref, o_ref, *, alpha):
    attn = _transpose_bf16(at_ref[0]).astype(BF16)
    mix = _dot(attn, w_ref[...])
    o_ref[...] = _layer_norm(alpha * x_ref[...] + mix, lg_ref[...], lb_ref[...])


def _odd_out(x2d, attn_t, w, ln_g, ln_b, alpha):
    T, D = x2d.shape
    _, width, seq = attn_t.shape
    tm = min(ROW_TILE, seq)
    tiles = seq // tm
    row = lambda wd: pl.BlockSpec((tm, wd), lambda i: (i, 0))
    whole = lambda a: pl.BlockSpec(a.shape, lambda i: (0, 0))
    return pl.pallas_call(
        functools.partial(_odd_out_kernel, alpha=alpha),
        grid=(T // tm,),
        in_specs=[row(D), pl.BlockSpec((1, width, tm), lambda i: (i // tiles, 0, i % tiles)),
                  whole(w), whole(ln_g), whole(ln_b)],
        out_specs=row(D), out_shape=jax.ShapeDtypeStruct((T, D), F32),
        compiler_params=_cparams("parallel"), name="odd_out",
    )(x2d, attn_t, w, ln_g, ln_b)


def _eye(n):
    return (lax.broadcasted_iota(I32, (n, n), 0) == lax.broadcasted_iota(I32, (n, n), 1)).astype(BF16)


def _transpose_bf16(a):
    return _dot_nt(_eye(a.shape[1]), a)


def _moba_kernel(q_ref, k_ref, v_ref, o_ref, kmean_ref, vt_ref, sel_ref, s_ref, p_ref, *, seq, top):
    BLK, dh, NH = MOBA_BLOCK, HEAD_DIM, MOBA_HEADS_PER_STEP
    nb = seq // BLK
    i = pl.program_id(2)

    @pl.when(i == 0)
    def _():
        kmean_ref[...] = jnp.mean(k_ref[0].astype(F32).reshape(nb, BLK, NH * dh), axis=1)
        for r in range(nb):
            cols = slice(r * BLK, (r + 1) * BLK)
            v_t = _transpose_bf16(v_ref[0, cols, :]).astype(BF16)
            for h in range(NH):
                vt_ref[h, 0:dh, cols] = v_t[h * dh:(h + 1) * dh]
        ones_row = jnp.where(lax.broadcasted_iota(I32, (VT_ROWS - dh, seq), 0) == 0, 1.0, 0.0).astype(BF16)
        for h in range(NH):
            vt_ref[h, dh:VT_ROWS, :] = ones_row

    jf = lax.broadcasted_iota(I32, (nb, BLK), 0).astype(F32)
    i_f = i.astype(F32)
    causal = lax.broadcasted_iota(I32, (BLK, BLK), 0) <= lax.broadcasted_iota(I32, (BLK, BLK), 1)

    heads = tuple(slice(h * dh, (h + 1) * dh) for h in range(NH))
    qs = [q_ref[0, :, hs] * jnp.asarray(dh ** -0.5, BF16) for hs in heads]

    def scores(j, h):
        c0 = pl.multiple_of(j * BLK, BLK)
        return _dot_nt(k_ref[0, pl.ds(c0, BLK), heads[h]], qs[h])

    def values(j, h, p):
        c0 = pl.multiple_of(j * BLK, BLK)
        return _dot(vt_ref[h, :, pl.ds(c0, BLK)], p)

    gates = []
    for h in range(NH):
        km_hi, km_lo = _split(kmean_ref[:, heads[h]])
        gates.append(_dot_nt(km_hi, qs[h]) + _dot_nt(km_lo, qs[h]))
    s_own = [scores(i, h) for h in range(NH)]
    s_first = [scores(0, h) for h in range(NH)]

    start = []
    for h in range(NH):
        gate = jnp.where(jf < i_f, gates[h], NEG)
        selm = jnp.zeros((nb, BLK), F32)
        for _ in range(top):
            m = jnp.max(gate, axis=0, keepdims=True)
            idx = jnp.min(jnp.where(gate == m, jf, float(nb)), axis=0, keepdims=True)
            hit = jf == idx
            selm = jnp.where(hit, jnp.where(m > 0.5 * NEG, 1.0, 0.0), selm)
            gate = jnp.where(hit, 2.0 * NEG, gate)
        sel_ref[h] = jnp.concatenate([selm, jnp.ones((SUBLANES, BLK), F32)], axis=0)
        sm = jnp.where(causal, s_own[h], NEG)
        m0 = jnp.max(sm, axis=0, keepdims=True)
        p = jnp.exp(sm - m0)
        s_ref[0, h] = s_first[h]
        p_ref[0, h] = p.astype(BF16)
        start += [m0, jnp.zeros((VT_ROWS, BLK), F32)]

    def prev_of(t):
        return jnp.where(t == 0, i, t - 1), jnp.where(t == 0, nb, t - 1)

    def stage(t, src, dst, carry):
        blk_prev, row_prev = prev_of(t)
        nxt = jnp.minimum(t + 1, nb - 1)
        for h in range(NH):
            s_ref[dst, h] = scores(nxt, h)
        pv = [values(blk_prev, h, p_ref[src, h]) for h in range(NH)]
        out = []
        for h in range(NH):
            m_i, acc = carry[2 * h:2 * h + 2]
            on = sel_ref[h, pl.ds(t, 1), :] > 0.5
            on_prev = sel_ref[h, pl.ds(row_prev, 1), :] > 0.5
            m_new = jnp.where(on, jnp.maximum(m_i, jnp.max(s_ref[src, h], axis=0, keepdims=True)), m_i)
            p_ref[dst, h] = jnp.exp(s_ref[src, h] - m_new).astype(BF16)
            out += [m_new, jnp.exp(m_i - m_new) * (acc + jnp.where(on_prev, pv[h], 0.0))]
        return tuple(out)

    n_pairs = (i + 1) // 2
    fin = lax.fori_loop(0, n_pairs, lambda u, c: stage(2 * u + 1, 1, 0, stage(2 * u, 0, 1, c)), tuple(start))
    blk_prev, row_prev = prev_of(2 * n_pairs)
    outs = []
    for h in range(NH):
        on_prev = sel_ref[h, pl.ds(row_prev, 1), :] > 0.5
        acc = fin[2 * h + 1] + jnp.where(on_prev, values(blk_prev, h, p_ref[0, h]), 0.0)
        outs.append(acc[0:dh] / acc[dh:dh + 1])
    o_ref[0] = jnp.concatenate(outs, axis=0).astype(o_ref.dtype)


def _moba_attention(q, k, v, seq):
    B, _, width = q.shape
    BLK, NH = MOBA_BLOCK, MOBA_HEADS_PER_STEP
    hw = NH * HEAD_DIM
    nb = seq // BLK
    top = min(MOBA_TOPK, max(nb - 1, 1))
    return pl.pallas_call(
        functools.partial(_moba_kernel, seq=seq, top=top),
        grid=(B, width // hw, nb),
        in_specs=[pl.BlockSpec((1, BLK, hw), lambda b, h, i: (b, i, h)),
                  pl.BlockSpec((1, seq, hw), lambda b, h, i: (b, 0, h)),
                  pl.BlockSpec((1, seq, hw), lambda b, h, i: (b, 0, h))],
        out_specs=pl.BlockSpec((1, hw, BLK), lambda b, h, i: (b, h, i)),
        out_shape=jax.ShapeDtypeStruct((B, width, seq), BF16),
        scratch_shapes=[pltpu.VMEM((nb, hw), F32), pltpu.VMEM((NH, VT_ROWS, seq), BF16),
                        pltpu.VMEM((NH, nb + SUBLANES, BLK), F32),
                        pltpu.VMEM((2, NH, BLK, BLK), F32), pltpu.VMEM((2, NH, BLK, BLK), BF16)],
        compiler_params=_cparams("parallel", "parallel", "arbitrary"), name="moba_attn",
    )(q, k, v)


def _route(logits):
    n_fine = MOE_GROUPS * MOE_EXPERTS_PER_GROUP
    lane = lax.broadcasted_iota(I32, logits.shape, 1)
    lf = lane.astype(F32)
    is_coarse = (lane >= n_fine) & (lane < n_fine + MOE_GROUPS)
    pc = _masked_softmax(logits, is_coarse)
    gv = jnp.max(pc, axis=-1, keepdims=True)
    gidx = jnp.min(jnp.where(is_coarse & (pc == gv), lf - float(n_fine), float(MOE_GROUPS)), axis=-1, keepdims=True)
    in_group = (lane < n_fine) & ((lane // MOE_EXPERTS_PER_GROUP).astype(F32) == gidx)
    pf = _masked_softmax(logits, in_group)
    cand = jnp.where(in_group, pf, -1.0)
    m1 = jnp.max(cand, axis=-1, keepdims=True)
    i1 = jnp.min(jnp.where(cand == m1, lf, float(LANES)), axis=-1, keepdims=True)
    cand = jnp.where(lf == i1, -1.0, cand)
    m2 = jnp.max(cand, axis=-1, keepdims=True)
    i2 = jnp.min(jnp.where(cand == m2, lf, float(LANES)), axis=-1, keepdims=True)
    tot = m1 + m2
    return jnp.where(lf == i1, gv * (m1 / tot), jnp.where(lf == i2, gv * (m2 / tot), 0.0))


def _moe_kernel(x_ref, wr_ref, br_ref, wg_ref, wu_ref, wd_ref, lg_ref, lb_ref, o_ref, gates_ref, acc_ref, *, alpha):
    E, FF = MOE_EXPERTS_PER_GROUP, MOE_FF
    g = pl.program_id(1)
    x = x_ref[...]

    @pl.when(g == 0)
    def _():
        gates_ref[...] = _route(_dot3(x, wr_ref[...]) + br_ref[...])
        acc_ref[...] = jnp.zeros_like(acc_ref)

    xb = x.astype(BF16)
    hg = _dot(xb, wg_ref[0])
    hu = _dot(xb, wu_ref[0])
    h = hg * _sigmoid(hg) * hu
    gates = gates_ref[...]
    lane = lax.broadcasted_iota(I32, gates.shape, 1)
    parts = []
    for e in range(E):
        ge = jnp.sum(jnp.where(lane == g * E + e, gates, 0.0), axis=-1, keepdims=True)
        parts.append((h[:, e * FF:(e + 1) * FF] * ge).astype(BF16))
    acc_ref[...] += _dot(jnp.concatenate(parts, axis=-1), wd_ref[0])

    @pl.when(g == pl.num_programs(1) - 1)
    def _():
        o_ref[...] = _layer_norm(alpha * x + acc_ref[...], lg_ref[...], lb_ref[...])


def _moe(x2d, wr, br, wg, wu, wd, ln_g, ln_b, alpha):
    T, D = x2d.shape
    tm = min(ROW_TILE, T)
    NG = wg.shape[0]
    whole = lambda a: pl.BlockSpec(a.shape, lambda i, g: (0, 0))
    per_g = lambda a: pl.BlockSpec((1,) + a.shape[1:], lambda i, g: (g, 0, 0))
    return pl.pallas_call(
        functools.partial(_moe_kernel, alpha=alpha),
        grid=(T // tm, NG),
        in_specs=[pl.BlockSpec((tm, D), lambda i, g: (i, 0)), whole(wr), whole(br), per_g(wg), per_g(wu), per_g(wd),
                  whole(ln_g), whole(ln_b)],
        out_specs=pl.BlockSpec((tm, D), lambda i, g: (i, 0)),
        out_shape=jax.ShapeDtypeStruct((T, D), F32),
        scratch_shapes=[pltpu.VMEM((tm, LANES), F32), pltpu.VMEM((tm, D), F32)],
        compiler_params=_cparams("parallel", "arbitrary"), name="moe",
    )(x2d, wr, br, wg, wu, wd, ln_g, ln_b)


def _rope_tables(seq):
    inv = 1.0 / (ROPE_THETA ** (jnp.arange(0, HEAD_DIM, 2, dtype=F32) / HEAD_DIM))
    ang = jnp.arange(seq, dtype=F32)[:, None] * inv[None, :]
    cos, sin = jnp.cos(ang), jnp.sin(ang)
    reps = LANES // HEAD_DIM
    return jnp.tile(jnp.concatenate([cos, cos], -1), (1, reps)), jnp.tile(jnp.concatenate([-sin, sin], -1), (1, reps))


def _selection_constants(seq):
    ncp = seq // CMP_STRIDE
    nbs = seq // SEL_BLOCK
    starts = jnp.arange(ncp) * CMP_STRIDE
    bstart = jnp.arange(nbs) * SEL_BLOCK
    ovl = jnp.clip(jnp.minimum(starts[:, None] + CMP_BLOCK, bstart[None, :] + SEL_BLOCK)
                   - jnp.maximum(starts[:, None], bstart[None, :]), 0, CMP_BLOCK)
    return (ovl.astype(F32) / CMP_BLOCK).astype(BF16).T


def _moe_weights(w_coarse, b_coarse, w_fine, b_fine, w_gate, w_up, w_down):
    D = w_coarse.shape[0]
    NG, E, _, FF = w_gate.shape
    n_fine = NG * E
    wr = jnp.zeros((D, LANES), F32)
    wr = wr.at[:, :n_fine].set(w_fine.transpose(1, 0, 2).reshape(D, n_fine)).at[:, n_fine:n_fine + NG].set(w_coarse)
    br = jnp.zeros((1, LANES), F32)
    br = br.at[0, :n_fine].set(b_fine.reshape(n_fine)).at[0, n_fine:n_fine + NG].set(b_coarse)
    wg = w_gate.transpose(0, 2, 1, 3).reshape(NG, D, E * FF).astype(BF16)
    wu = w_up.transpose(0, 2, 1, 3).reshape(NG, D, E * FF).astype(BF16)
    wd = w_down.reshape(NG, E * FF, D).astype(BF16)
    return wr, br, wg, wu, wd


def _even_mixer(x2d, B, S, cos2, sin2, w_in, pe_k, pe_v, k_w1, k_w2, v_w1, v_w2,
                lam_re, lam_im, log_step, b_re, b_im, c_re, c_im, d_skip, glu_w, glu_b, w_out, ln_g, ln_b, alpha):
    G, Z, dh = NSA_KV_GROUPS, NSA_HEADS_PER_GROUP, HEAD_DIM
    kvw = NSA_KV_WIDTH
    D = x2d.shape[1]
    cuts = [NSA_WIDTH + i * kvw for i in range(7)]
    w_q, w_kc, w_vc, w_ks, w_vs, w_kw, w_vw = [w_in[:, a:b] for a, b in zip([0] + cuts[:-1], cuts)]
    n_gate = 3 * NSA_HEADS
    w_g = jnp.pad(w_in[:, cuts[-1]:cuts[-1] + n_gate], ((0, 0), (0, LANES - n_gate)))
    w_u = w_in[:, cuts[-1] + n_gate:]
    weights = [w_q.astype(BF16), jnp.concatenate([w_kc, w_ks, w_kw], 1).astype(BF16),
               jnp.concatenate([w_vc, w_vs, w_vw], 1).astype(BF16), w_u.astype(BF16), w_g.astype(BF16)]
    q, kk, vv, u, gates = _project(x2d, cos2, sin2, weights,
                                   ["rope", "rope_heads", "plain_heads", "plain", "sigmoid"],
                                   [BF16, BF16, BF16, F32, F32], S)

    kc, ks, kw = [kk[i * G:(i + 1) * G].reshape(G, B, S, dh) for i in range(3)]
    vc, vs, vw = [vv[i * G:(i + 1) * G].reshape(G, B, S, dh) for i in range(3)]
    gates_t = gates[:, :n_gate].reshape(B, S, G, 3 * Z).transpose(0, 2, 3, 1)

    n_rows = S // CMP_STRIDE
    r = jnp.stack([kc, vc]).reshape(2, G * B, n_rows, CMP_STRIDE * dh)
    pe = jnp.stack([pe_k, pe_v]).reshape(2, 1, CMP_BLOCK * dh)
    w1 = jnp.stack([k_w1, v_w1]).astype(BF16)
    w2 = jnp.stack([k_w2, v_w2]).astype(BF16)
    cmp, cmp_t = _nsa_compress(r, pe, w1, w2, w2.transpose(0, 2, 1))
    kcm = cmp[0].reshape(G, B, n_rows, dh)
    vct = cmp_t[1].reshape(G, B, dh, n_rows)

    o_nsa = _nsa_attention(q.reshape(B, S, NSA_WIDTH), kcm, vct, ks, vs, kw, vw, gates_t, _selection_constants(S), S)

    C, L = S5_GROUP_CH, S5_CHUNK
    SG = u.shape[1] // C
    mt, pm, qt, al = _s5_operators(lam_re[:, None, :], lam_im[:, None, :], log_step[:, None, None],
                                   b_re.transpose(0, 2, 1), b_im.transpose(0, 2, 1), c_re, c_im)
    u_g = u.reshape(B, S // L, L, SG, C).transpose(3, 1, 0, 2, 4).reshape(SG, (S // L) * B, L * C)
    d_t = jnp.tile(d_skip.reshape(SG, 1, C), (1, 1, L))
    y = _s5_scan(u_g, mt, pm, qt, al, d_t, B)
    y = y.reshape(SG, S // L, B, L, C).transpose(2, 1, 3, 0, 4).reshape(B * S, SG * C)

    return _even_out(x2d, o_nsa.reshape(B * S, NSA_WIDTH), y, glu_w.astype(BF16), glu_b[None, :],
                     w_out[:NSA_WIDTH].astype(BF16), w_out[NSA_WIDTH:].astype(BF16), ln_g[None, :], ln_b[None, :], alpha)


def _odd_mixer(x2d, B, S, cos2, sin2, w_in, w_out, ln_g, ln_b, alpha):
    D = x2d.shape[1]
    q, k, v = _project(x2d, cos2, sin2, [w_in[:, i * D:(i + 1) * D].astype(BF16) for i in range(3)],
                       ["rope", "rope", "plain"], [BF16, BF16, BF16], S)
    attn_t = _moba_attention(q.reshape(B, S, D), k.reshape(B, S, D), v.reshape(B, S, D), S)
    return _odd_out(x2d, attn_t, w_out.astype(BF16), ln_g[None, :], ln_b[None, :], alpha)


def kernel(x, ev_w_in, nsa_pe_k, nsa_pe_v, nsa_cmp_k_w1, nsa_cmp_k_w2, nsa_cmp_v_w1, nsa_cmp_v_w2, s5_lambda_re, s5_lambda_im, s5_log_step, s5_b_re, s5_b_im, s5_c_re, s5_c_im, s5_d, s5_glu_w, s5_glu_b, ev_w_out, od_w_in, od_w_out, ln_mix_g, ln_mix_b, ln_ffn_g, ln_ffn_b, moe_w_coarse, moe_b_coarse, moe_w_fine, moe_b_fine, moe_w_gate, moe_w_up, moe_w_down):
    B, S, D = x.shape
    depth = ln_mix_g.shape[0]
    alpha = (2.0 * depth) ** 0.25
    cos2, sin2 = _rope_tables(S)
    h = x.reshape(B * S, D)
    for layer in range(depth):
        if layer % 2 == 0:
            e = layer // 2
            h = _even_mixer(h, B, S, cos2, sin2, ev_w_in[e], nsa_pe_k[e], nsa_pe_v[e], nsa_cmp_k_w1[e], nsa_cmp_k_w2[e],
                            nsa_cmp_v_w1[e], nsa_cmp_v_w2[e], s5_lambda_re[e], s5_lambda_im[e], s5_log_step[e],
                            s5_b_re[e], s5_b_im[e], s5_c_re[e], s5_c_im[e], s5_d[e], s5_glu_w[e], s5_glu_b[e],
                            ev_w_out[e], ln_mix_g[layer], ln_mix_b[layer], alpha)
        else:
            o = layer // 2
            h = _odd_mixer(h, B, S, cos2, sin2, od_w_in[o], od_w_out[o], ln_mix_g[layer], ln_mix_b[layer], alpha)
        wr, br, wg, wu, wd = _moe_weights(moe_w_coarse[layer], moe_b_coarse[layer], moe_w_fine[layer], moe_b_fine[layer],
                                          moe_w_gate[layer], moe_w_up[layer], moe_w_down[layer])
        h = _moe(h, wr, br, wg, wu, wd, ln_ffn_g[layer][None, :], ln_ffn_b[layer][None, :], alpha)
    return h.reshape(B, S, D)
```

```python
import functools
import math

import jax
import jax.numpy as jnp
from jax import lax
from jax.experimental import pallas as pl
from jax.experimental.pallas import tpu as pltpu

F32 = jnp.float32
BF16 = jnp.bfloat16
I32 = jnp.int32

HEAD_DIM = 64
ROPE_THETA = 10000.0
LN_EPS = 1e-5

NSA_KV_GROUPS = 2
NSA_HEADS_PER_GROUP = 4
NSA_HEADS = NSA_KV_GROUPS * NSA_HEADS_PER_GROUP
NSA_WIDTH = NSA_HEADS * HEAD_DIM
NSA_KV_WIDTH = NSA_KV_GROUPS * HEAD_DIM
CMP_BLOCK = 32
CMP_STRIDE = 16
SEL_BLOCK = 64
SEL_TOPN = 8
WINDOW = 256
NSA_QCHUNK = 128

S5_GROUP_CH = 16
S5_STATE = 64
S5_CHUNK = 64

MOBA_BLOCK = 256
MOBA_TOPK = 3
MOBA_HEADS_PER_STEP = 8

MOE_GROUPS = 4
MOE_EXPERTS_PER_GROUP = 8
MOE_FF = 128

LANES = 128
SUBLANES = 8
MXU_COLS = 256
VT_ROWS = HEAD_DIM + 16
ROW_TILE = 512
MOE_ROW_TILE = 1024
VMEM_LIMIT = 56 * 1024 * 1024
NEG = -1e30


def _cparams(*sem):
    return pltpu.CompilerParams(dimension_semantics=sem, vmem_limit_bytes=VMEM_LIMIT)


def _dot(a, b):
    return jnp.dot(a, b, preferred_element_type=F32)


def _dot_nt(a, b):
    return lax.dot_general(a, b, (((1,), (1,)), ((), ())), preferred_element_type=F32)


def _split(a):
    hi = a.astype(BF16)
    lo = (a - hi.astype(F32)).astype(BF16)
    return hi, lo


def _dot3(a, b):
    ah, al = _split(a)
    bh, bl = _split(b)
    return _dot(ah, bh) + _dot(ah, bl) + _dot(al, bh)


def _dot3_nt(a, b):
    ah, al = _split(a)
    bh, bl = _split(b)
    return _dot_nt(ah, bh) + _dot_nt(ah, bl) + _dot_nt(al, bh)


def _sigmoid(x):
    return 1.0 / (1.0 + jnp.exp(-x))


def _gelu_tanh(x):
    return 0.5 * x * (1.0 + jnp.tanh(math.sqrt(2.0 / math.pi) * (x + 0.044715 * (x * x * x))))


def _layer_norm(r, g, b):
    mu = jnp.mean(r, axis=-1, keepdims=True)
    d = r - mu
    var = jnp.mean(d * d, axis=-1, keepdims=True)
    return d * lax.rsqrt(var + LN_EPS) * g + b


def _masked_softmax(s, ok):
    sm = jnp.where(ok, s, NEG)
    m = jnp.max(sm, axis=-1, keepdims=True)
    p = jnp.where(ok, jnp.exp(sm - m), 0.0)
    return p / jnp.maximum(jnp.sum(p, axis=-1, keepdims=True), 1e-30)


def _rope_128(a, cos2, sin2):
    lane = lax.broadcasted_iota(I32, a.shape, 1)
    first_half = (lane & (HEAD_DIM - 1)) < (HEAD_DIM // 2)
    partner = jnp.where(first_half, pltpu.roll(a, LANES - HEAD_DIM // 2, 1), pltpu.roll(a, HEAD_DIM // 2, 1))
    return a * cos2 + partner * sin2


def _proj_kernel(x_ref, cos_ref, sin_ref, *refs, kinds):
    n = len(kinds)
    w_refs, o_refs = refs[:n], refs[n:]
    xb = x_ref[...].astype(BF16)
    cos2 = cos_ref[...]
    sin2 = sin_ref[...]
    for kind, w_ref, o_ref in zip(kinds, w_refs, o_refs):
        width = w_ref.shape[1]
        for c0 in range(0, width, MXU_COLS):
            cw = min(MXU_COLS, width - c0)
            acc = _dot(xb, w_ref[:, c0:c0 + cw])
            for l0 in range(0, cw, LANES):
                part = acc[:, l0:l0 + LANES]
                if kind.startswith("rope"):
                    part = _rope_128(part, cos2, sin2)
                elif kind == "sigmoid":
                    part = _sigmoid(part)
                if kind.endswith("_heads"):
                    hd = (c0 + l0) // HEAD_DIM
                    o_ref[hd] = part[:, :HEAD_DIM].astype(o_ref.dtype)
                    o_ref[hd + 1] = part[:, HEAD_DIM:].astype(o_ref.dtype)
                else:
                    o_ref[:, c0 + l0:c0 + l0 + LANES] = part.astype(o_ref.dtype)


def _project(x2d, cos2, sin2, weights, kinds, dtypes, seq):
    T, D = x2d.shape
    tm = min(ROW_TILE, seq)
    pos_tiles = seq // tm
    in_specs = [pl.BlockSpec((tm, D), lambda i: (i, 0)),
                pl.BlockSpec((tm, LANES), lambda i: (i % pos_tiles, 0)),
                pl.BlockSpec((tm, LANES), lambda i: (i % pos_tiles, 0))]
    in_specs += [pl.BlockSpec(w.shape, lambda i: (0, 0)) for w in weights]
    out_specs, out_shape = [], []
    for w, kind, dt in zip(weights, kinds, dtypes):
        if kind.endswith("_heads"):
            n_heads = w.shape[1] // HEAD_DIM
            out_specs.append(pl.BlockSpec((n_heads, tm, HEAD_DIM), lambda i: (0, i, 0)))
            out_shape.append(jax.ShapeDtypeStruct((n_heads, T, HEAD_DIM), dt))
        else:
            out_specs.append(pl.BlockSpec((tm, w.shape[1]), lambda i: (i, 0)))
            out_shape.append(jax.ShapeDtypeStruct((T, w.shape[1]), dt))
    return pl.pallas_call(
        functools.partial(_proj_kernel, kinds=tuple(kinds)),
        grid=(T // tm,), in_specs=in_specs, out_specs=out_specs, out_shape=out_shape,
        compiler_params=_cparams("parallel"), name="proj",
    )(x2d, cos2, sin2, *weights)


def _compress_kernel(r_ref, pe_ref, w1_ref, w2_ref, w2t_ref, o_ref, ot_ref):
    half = CMP_STRIDE * HEAD_DIM
    r = r_ref[0, 0]
    n = r.shape[0]
    top = _dot(r, w1_ref[0, :half, :])
    bot = _dot(r, w1_ref[0, half:, :])
    bias = _dot(pe_ref[0].astype(BF16), w1_ref[0])
    h = top + pltpu.roll(bot, n - 1, 0) + bias
    hb = _gelu_tanh(h).astype(BF16)
    o_ref[0, 0] = _dot(hb, w2_ref[0]).astype(o_ref.dtype)
    ot_ref[0, 0] = _dot_nt(w2t_ref[0], hb).astype(ot_ref.dtype)


def _nsa_compress(r, pe, w1, w2, w2t):
    _, bg, n, width = r.shape
    hid = w1.shape[2]
    return pl.pallas_call(
        _compress_kernel,
        grid=(2, bg),
        in_specs=[pl.BlockSpec((1, 1, n, width), lambda a, i: (a, i, 0, 0)),
                  pl.BlockSpec((1, 1, 2 * width), lambda a, i: (a, 0, 0)),
                  pl.BlockSpec((1, 2 * width, hid), lambda a, i: (a, 0, 0)),
                  pl.BlockSpec((1, hid, HEAD_DIM), lambda a, i: (a, 0, 0)),
                  pl.BlockSpec((1, HEAD_DIM, hid), lambda a, i: (a, 0, 0))],
        out_specs=[pl.BlockSpec((1, 1, n, HEAD_DIM), lambda a, i: (a, i, 0, 0)),
                   pl.BlockSpec((1, 1, HEAD_DIM, n), lambda a, i: (a, i, 0, 0))],
        out_shape=[jax.ShapeDtypeStruct((2, bg, n, HEAD_DIM), BF16),
                   jax.ShapeDtypeStruct((2, bg, HEAD_DIM, n), BF16)],
        compiler_params=_cparams("parallel", "parallel"), name="nsa_compress",
    )(r, pe, w1, w2, w2t)


def _exp_cols(s, ok):
    sm = jnp.where(ok, s, NEG)
    m = jnp.max(sm, axis=0, keepdims=True)
    p = jnp.exp(sm - jnp.where(m > 0.5 * NEG, m, 0.0))
    return p, jnp.sum(p, axis=0, keepdims=True)


def _nsa_attn_kernel(q_ref, kc_ref, vct_ref, ks_ref, vs_ref, kw_ref, vw_ref, gt_ref, c2st_ref, o_ref,
                     vst_ref, vwt_ref, sel_ref, s_ref, p_ref, *, seq, n_sel):
    Z, dh, QC, SB = NSA_HEADS_PER_GROUP, HEAD_DIM, NSA_QCHUNK, SEL_BLOCK
    W = Z * QC
    KT = 2 * SB
    ncp = seq // CMP_STRIDE
    nbs = seq // SEL_BLOCK
    c = pl.program_id(2)
    t0 = c * QC

    @pl.when(c == 0)
    def _():
        step = 4 * QC
        for r in range(seq // step):
            rows = slice(r * step, (r + 1) * step)
            vst_ref[0:dh, rows] = _transpose_bf16(vs_ref[0, 0, rows, :]).astype(BF16)
            vwt_ref[0:dh, rows] = _transpose_bf16(vw_ref[0, 0, rows, :]).astype(BF16)
        ones_row = jnp.where(lax.broadcasted_iota(I32, (VT_ROWS - dh, seq), 0) == 0, 1.0, 0.0).astype(BF16)
        vst_ref[dh:VT_ROWS, :] = ones_row
        vwt_ref[dh:VT_ROWS, :] = ones_row

    q_t = _transpose_bf16(q_ref[0]).astype(BF16)
    qt = jnp.concatenate([q_t[z * dh:(z + 1) * dh] for z in range(Z)], axis=1)
    qt = qt * jnp.asarray(dh ** -0.5, BF16)
    qpos = t0 + (lax.broadcasted_iota(I32, (1, W), 1) & (QC - 1))
    d0 = pl.multiple_of(t0, QC)
    band = QC + WINDOW
    w0 = pl.multiple_of(jnp.maximum(t0 - WINDOW, 0), QC)

    def tile4(row):
        return jnp.concatenate([row] * Z, axis=1)

    s_c = _dot(kc_ref[0, 0], qt)
    s_w = _dot(kw_ref[0, 0, pl.ds(w0, band), :], qt)
    s_d = _dot(ks_ref[0, 0, pl.ds(d0, KT), :], qt)
    s_ref[0] = _dot(ks_ref[0, 0, pl.ds(0, KT), :], qt)

    cmp_end = lax.broadcasted_iota(I32, (ncp, 1), 0) * CMP_STRIDE + (CMP_BLOCK - 1)
    e_c, l_c = _exp_cols(s_c, cmp_end <= qpos)
    p_c = e_c * (1.0 / jnp.maximum(l_c, 1e-30))
    o_c = _dot(vct_ref[0, 0], p_c.astype(BF16))

    p_sum = p_c[:, 0:QC]
    for z in range(1, Z):
        p_sum = p_sum + p_c[:, z * QC:(z + 1) * QC]
    p_hi, p_lo = _split(p_sum)
    imp = _dot(c2st_ref[...], p_hi) + _dot(c2st_ref[...], p_lo)

    jf = lax.broadcasted_iota(I32, (nbs, QC), 0).astype(F32)
    qblk = (qpos[:, :QC] // SEL_BLOCK).astype(F32)
    future = jf > qblk
    forced = (jf == 0.0) | (jf == qblk) | (jf == qblk - 1.0)
    score = jnp.where(future, -1.0, jnp.where(forced, 1e3, imp))
    selm = jnp.zeros((nbs, QC), F32)
    for _ in range(n_sel):
        m = jnp.max(score, axis=0, keepdims=True)
        idx = jnp.min(jnp.where(score == m, jf, float(nbs)), axis=0, keepdims=True)
        hit = jf == idx
        selm = jnp.where(hit, jnp.where(m >= 0.0, 1.0, 0.0), selm)
        score = jnp.where(hit, NEG, score)
    selw = tile4(selm)
    sel_ref[...] = jnp.where(lax.broadcasted_iota(I32, (nbs, W), 0) < 2 * c, selw, 0.0)

    brow = lax.broadcasted_iota(I32, (nbs, 1), 0)
    on_a = jnp.sum(jnp.where(brow == 2 * c, selw, 0.0), axis=0, keepdims=True)
    on_b = jnp.sum(jnp.where(brow == 2 * c + 1, selw, 0.0), axis=0, keepdims=True)
    krow = lax.broadcasted_iota(I32, (KT, 1), 0)
    ok_d = jnp.where(t0 + krow <= qpos, jnp.where(krow < SB, on_a, on_b), 0.0) > 0.5
    sm = jnp.where(ok_d, s_d, NEG)
    m0 = jnp.max(sm, axis=0, keepdims=True)
    p_d = jnp.exp(sm - m0)
    acc0 = _dot(vst_ref[:, pl.ds(d0, KT)], p_d.astype(BF16))
    p_ref[0] = jnp.zeros((KT, W), BF16)

    dist = qpos - (w0 + lax.broadcasted_iota(I32, (band, 1), 0))
    sm_w = jnp.where(jnp.where(dist >= 0, dist, WINDOW) < WINDOW, s_w, NEG)
    e_w = jnp.exp(sm_w - jnp.max(sm_w, axis=0, keepdims=True))
    acc_w = _dot(vwt_ref[:, pl.ds(w0, band)], e_w.astype(BF16))
    o_w = acc_w[0:dh] * (1.0 / jnp.maximum(acc_w[dh:dh + 1], 1e-30))

    def stage(t, src, dst, carry):
        m_i, acc = carry
        tp = jnp.maximum(t - 1, 0)
        k_next = pl.multiple_of(jnp.minimum(t + 1, c) * KT, KT)
        s_ref[dst] = _dot(ks_ref[0, 0, pl.ds(k_next, KT), :], qt)
        vt = vst_ref[:, pl.ds(pl.multiple_of(tp * KT, KT), KT)]
        pv_a = _dot(vt[:, :SB], p_ref[src, 0:SB, :])
        pv_b = _dot(vt[:, SB:], p_ref[src, SB:KT, :])
        on_a = sel_ref[pl.ds(2 * t, 1), :] > 0.5
        on_b = sel_ref[pl.ds(2 * t + 1, 1), :] > 0.5
        prev_a = sel_ref[pl.ds(2 * tp, 1), :] > 0.5
        prev_b = sel_ref[pl.ds(2 * tp + 1, 1), :] > 0.5
        mx_a = jnp.max(s_ref[src, 0:SB, :], axis=0, keepdims=True)
        mx_b = jnp.max(s_ref[src, SB:KT, :], axis=0, keepdims=True)
        m_new = jnp.maximum(m_i, jnp.maximum(jnp.where(on_a, mx_a, NEG), jnp.where(on_b, mx_b, NEG)))
        p_ref[dst] = jnp.exp(s_ref[src] - m_new).astype(BF16)
        acc_new = jnp.exp(m_i - m_new) * (acc + jnp.where(prev_a, pv_a, 0.0) + jnp.where(prev_b, pv_b, 0.0))
        return m_new, acc_new

    n_pairs = (c + 1) // 2
    _, acc_s = lax.fori_loop(0, n_pairs, lambda u, cr: stage(2 * u + 1, 1, 0, stage(2 * u, 0, 1, cr)), (m0, acc0))
    tl = jnp.maximum(2 * n_pairs - 1, 0)
    vt = vst_ref[:, pl.ds(pl.multiple_of(tl * KT, KT), KT)]
    last_a = sel_ref[pl.ds(2 * tl, 1), :] > 0.5
    last_b = sel_ref[pl.ds(2 * tl + 1, 1), :] > 0.5
    acc_s = (acc_s + jnp.where(last_a, _dot(vt[:, :SB], p_ref[0, 0:SB, :]), 0.0)
             + jnp.where(last_b, _dot(vt[:, SB:], p_ref[0, SB:KT, :]), 0.0))
    o_s = acc_s[0:dh] / jnp.maximum(acc_s[dh:dh + 1], 1e-30)

    gt = gt_ref[0, 0]
    g_c = jnp.concatenate([gt[3 * z:3 * z + 1] for z in range(Z)], axis=1)
    g_s = jnp.concatenate([gt[3 * z + 1:3 * z + 2] for z in range(Z)], axis=1)
    g_w = jnp.concatenate([gt[3 * z + 2:3 * z + 3] for z in range(Z)], axis=1)
    o_t = (g_c * o_c + g_s * o_s + g_w * o_w).astype(BF16)
    o_ref[0] = jnp.concatenate([_transpose_bf16(o_t[:, z * QC:(z + 1) * QC]) for z in range(Z)],
                               axis=1).astype(o_ref.dtype)


def _nsa_attention(q, kc, vct, ks, vs, kw, vw, gates_t, c2st, seq):
    G, B, _, dh = ks.shape
    Z, QC = NSA_HEADS_PER_GROUP, NSA_QCHUNK
    W = Z * QC
    KT = 2 * SEL_BLOCK
    ncp = seq // CMP_STRIDE
    nbs = seq // SEL_BLOCK
    full = lambda shp: pl.BlockSpec((1, 1) + shp, lambda b, g, c: (g, b, 0, 0))
    return pl.pallas_call(
        functools.partial(_nsa_attn_kernel, seq=seq, n_sel=min(SEL_TOPN, nbs)),
        grid=(B, G, seq // QC),
        in_specs=[pl.BlockSpec((1, QC, Z * dh), lambda b, g, c: (b, c, g)),
                  full((ncp, dh)), full((dh, ncp)), full((seq, dh)), full((seq, dh)), full((seq, dh)), full((seq, dh)),
                  pl.BlockSpec((1, 1, gates_t.shape[2], QC), lambda b, g, c: (b, g, 0, c)),
                  pl.BlockSpec((nbs, ncp), lambda b, g, c: (0, 0))],
        out_specs=pl.BlockSpec((1, QC, Z * dh), lambda b, g, c: (b, c, g)),
        out_shape=jax.ShapeDtypeStruct((B, seq, G * Z * dh), BF16),
        scratch_shapes=[pltpu.VMEM((VT_ROWS, seq), BF16), pltpu.VMEM((VT_ROWS, seq), BF16),
                        pltpu.VMEM((nbs, W), F32), pltpu.VMEM((2, KT, W), F32), pltpu.VMEM((2, KT, W), BF16)],
        compiler_params=_cparams("parallel", "parallel", "arbitrary"), name="nsa_attn",
    )(q, kc, vct, ks, vs, kw, vw, gates_t, c2st)


def _cmul(ar, ai, br, bi):
    return ar * br - ai * bi, ar * bi + ai * br


def _s5_ops_kernel(lre_ref, lim_ref, lstep_ref, bret_ref, bimt_ref, cre_ref, cim_ref,
                   mt_ref, pm_ref, qt_ref, al_ref):
    L, C, P = S5_CHUNK, S5_GROUP_CH, S5_STATE
    lr = lre_ref[0]
    li = lim_ref[0]
    step = jnp.exp(lstep_ref[0])
    mag = jnp.exp(lr * step)
    ar = mag * jnp.cos(li * step)
    ai = mag * jnp.sin(li * step)
    den = lr * lr + li * li
    fr = ((ar - 1.0) * lr + ai * li) / den
    fi = (ai * lr - (ar - 1.0) * li) / den
    bbr, bbi = _cmul(fr, fi, bret_ref[0], bimt_ref[0])

    up_r, up_i = jnp.ones((1, 1, P), F32), jnp.zeros((1, 1, P), F32)
    dn_r, dn_i = up_r, up_i
    sr, si = ar.reshape(1, 1, P), ai.reshape(1, 1, P)
    a1r, a1i = sr, si
    n = 1
    while n < L:
        tr, ti = _cmul(up_r, up_i, sr, si)
        up_r, up_i = jnp.concatenate([up_r, tr], 0), jnp.concatenate([up_i, ti], 0)
        tr, ti = _cmul(dn_r, dn_i, sr, si)
        dn_r, dn_i = jnp.concatenate([tr, dn_r], 0), jnp.concatenate([ti, dn_i], 0)
        sr, si = _cmul(sr, si, sr, si)
        n *= 2
    al_ref[0] = jnp.concatenate([sr.reshape(1, P), si.reshape(1, P)], -1)

    cr = cre_ref[0][None]
    ci = cim_ref[0][None]
    nr, ni = _cmul(up_r, up_i, a1r, a1i)
    wr, wi = _cmul(cr, ci, nr, ni)
    qt_ref[0] = jnp.concatenate([wr.reshape(L * C, P), -wi.reshape(L * C, P)], -1)
    er, ei = _cmul(dn_r, dn_i, bbr[None], bbi[None])
    pm_ref[0] = jnp.concatenate([er.reshape(L * C, P), ei.reshape(L * C, P)], -1).astype(pm_ref.dtype)
    w0r, w0i = _cmul(cr, ci, up_r, up_i)
    kt = _dot3_nt(bbr, w0r.reshape(L * C, P)) - _dot3_nt(bbi, w0i.reshape(L * C, P))
    lane = lax.broadcasted_iota(I32, kt.shape, 1)
    mt_ref[0, 0:C, :] = kt.astype(mt_ref.dtype)
    for s in range(1, L):
        shifted = jnp.where(lane >= s * C, pltpu.roll(kt, s * C, 1), 0.0)
        mt_ref[0, s * C:(s + 1) * C, :] = shifted.astype(mt_ref.dtype)


def _s5_operators(lre, lim, lstep, bret, bimt, cre, cim):
    G = lre.shape[0]
    L, C, P = S5_CHUNK, S5_GROUP_CH, S5_STATE
    vec = pl.BlockSpec((1, 1, P), lambda g: (g, 0, 0))
    mat = pl.BlockSpec((1, C, P), lambda g: (g, 0, 0))
    return pl.pallas_call(
        _s5_ops_kernel,
        grid=(G,),
        in_specs=[vec, vec, pl.BlockSpec((1, 1, 1), lambda g: (g, 0, 0)), mat, mat, mat, mat],
        out_specs=[pl.BlockSpec((1, L * C, L * C), lambda g: (g, 0, 0)),
                   pl.BlockSpec((1, L * C, 2 * P), lambda g: (g, 0, 0)),
                   pl.BlockSpec((1, L * C, 2 * P), lambda g: (g, 0, 0)),
                   pl.BlockSpec((1, 1, 2 * P), lambda g: (g, 0, 0))],
        out_shape=[jax.ShapeDtypeStruct((G, L * C, L * C), BF16),
                   jax.ShapeDtypeStruct((G, L * C, 2 * P), BF16),
                   jax.ShapeDtypeStruct((G, L * C, 2 * P), F32),
                   jax.ShapeDtypeStruct((G, 1, 2 * P), F32)],
        compiler_params=_cparams("parallel"), name="s5_operators",
    )(lre, lim, lstep, bret, bimt, cre, cim)


def _s5_scan_kernel(u_ref, mt_ref, pm_ref, qt_ref, al_ref, d_ref, y_ref, xin_ref, *, batch):
    P = S5_STATE
    rows = u_ref.shape[1]
    u = u_ref[0]
    ub = u.astype(BF16)
    y_ref[0] = _dot(ub, mt_ref[0]) + d_ref[0] * u
    xloc = _dot(ub, pm_ref[0])
    al = al_ref[0]
    lane = lax.broadcasted_iota(I32, (1, 2 * P), 1)
    mul_same = jnp.where(lane < P, al, pltpu.roll(al, P, 1))
    mul_swap = jnp.where(lane < P, -pltpu.roll(al, P, 1), al)
    xin_ref[...] = xloc

    def carry_step(k, state):
        r0 = pl.multiple_of(k * batch, batch)
        loc = xin_ref[pl.ds(r0, batch), :]
        xin_ref[pl.ds(r0, batch), :] = state
        return state * mul_same + pltpu.roll(state, P, 1) * mul_swap + loc

    lax.fori_loop(0, rows // batch, carry_step, jnp.zeros((batch, 2 * P), F32))
    y_ref[0] += _dot3_nt(xin_ref[...], qt_ref[0])


def _s5_scan(u, mt, pm, qt, al, d, batch):
    G, rows, width = u.shape
    P2 = 2 * S5_STATE
    per_g = lambda shp: pl.BlockSpec((1,) + shp, lambda g: (g, 0, 0))
    return pl.pallas_call(
        functools.partial(_s5_scan_kernel, batch=batch),
        grid=(G,),
        in_specs=[per_g((rows, width)), per_g((width, width)), per_g((width, P2)), per_g((width, P2)),
                  per_g((1, P2)), per_g((1, width))],
        out_specs=per_g((rows, width)),
        out_shape=jax.ShapeDtypeStruct((G, rows, width), F32),
        scratch_shapes=[pltpu.VMEM((rows, P2), F32)],
        compiler_params=_cparams("parallel"), name="s5_scan",
    )(u, mt, pm, qt, al, d)


def _even_out_kernel(x_ref, nsa_ref, y_ref, gw_ref, gb_ref, wa_ref, wb_ref, lg_ref, lb_ref, o_ref, *, alpha):
    g = _gelu_tanh(y_ref[...])
    s5 = g * _sigmoid(_dot(g.astype(BF16), gw_ref[...]) + gb_ref[...])
    mix = _dot(nsa_ref[...], wa_ref[...]) + _dot(s5.astype(BF16), wb_ref[...])
    o_ref[...] = _layer_norm(alpha * x_ref[...] + mix, lg_ref[...], lb_ref[...])


def _even_out(x2d, o_nsa, y_s5, glu_w, glu_b, wa, wb, ln_g, ln_b, alpha):
    T, D = x2d.shape
    tm = min(ROW_TILE, T)
    row = lambda w: pl.BlockSpec((tm, w), lambda i: (i, 0))
    whole = lambda a: pl.BlockSpec(a.shape, lambda i: (0, 0))
    return pl.pallas_call(
        functools.partial(_even_out_kernel, alpha=alpha),
        grid=(T // tm,),
        in_specs=[row(D), row(o_nsa.shape[1]), row(y_s5.shape[1]), whole(glu_w), whole(glu_b), whole(wa), whole(wb),
                  whole(ln_g), whole(ln_b)],
        out_specs=row(D), out_shape=jax.ShapeDtypeStruct((T, D), F32),
        compiler_params=_cparams("parallel"), name="even_out",
    )(x2d, o_nsa, y_s5, glu_w, glu_b, wa, wb, ln_g, ln_b)


def _odd_out_kernel(x_ref, at_ref, w_ref, lg_ref, lb_ref, o_ref, *, alpha):
    attn = _transpose_bf16(at_ref[0]).astype(BF16)
    mix = _dot(attn, w_ref[...])
    o_ref[...] = _layer_norm(alpha * x_ref[...] + mix, lg_ref[...], lb_ref[...])


def _odd_out(x2d, attn_t, w, ln_g, ln_b, alpha):
    T, D = x2d.shape
    _, width, seq = attn_t.shape
    tm = min(ROW_TILE, seq)
    tiles = seq // tm
    row = lambda wd: pl.BlockSpec((tm, wd), lambda i: (i, 0))
    whole = lambda a: pl.BlockSpec(a.shape, lambda i: (0, 0))
    return pl.pallas_call(
        functools.partial(_odd_out_kernel, alpha=alpha),
        grid=(T // tm,),
        in_specs=[row(D), pl.BlockSpec((1, width, tm), lambda i: (i // tiles, 0, i % tiles)),
                  whole(w), whole(ln_g), whole(ln_b)],
        out_specs=row(D), out_shape=jax.ShapeDtypeStruct((T, D), F32),
        compiler_params=_cparams("parallel"), name="odd_out",
    )(x2d, attn_t, w, ln_g, ln_b)


def _eye(n):
    return (lax.broadcasted_iota(I32, (n, n), 0) == lax.broadcasted_iota(I32, (n, n), 1)).astype(BF16)


def _transpose_bf16(a):
    return _dot_nt(_eye(a.shape[1]), a)


def _moba_kernel(q_ref, k_ref, v_ref, o_ref, kmean_ref, vt_ref, sel_ref, s_ref, p_ref, *, seq, top):
    BLK, dh, NH = MOBA_BLOCK, HEAD_DIM, MOBA_HEADS_PER_STEP
    nb = seq // BLK
    i = pl.program_id(2)

    @pl.when(i == 0)
    def _():
        kmean_ref[...] = jnp.mean(k_ref[0].astype(F32).reshape(nb, BLK, NH * dh), axis=1)
        for r in range(nb):
            cols = slice(r * BLK, (r + 1) * BLK)
            v_t = _transpose_bf16(v_ref[0, cols, :]).astype(BF16)
            for h in range(NH):
                vt_ref[h, 0:dh, cols] = v_t[h * dh:(h + 1) * dh]
        ones_row = jnp.where(lax.broadcasted_iota(I32, (VT_ROWS - dh, seq), 0) == 0, 1.0, 0.0).astype(BF16)
        for h in range(NH):
            vt_ref[h, dh:VT_ROWS, :] = ones_row

    jf = lax.broadcasted_iota(I32, (nb, BLK), 0).astype(F32)
    i_f = i.astype(F32)
    causal = lax.broadcasted_iota(I32, (BLK, BLK), 0) <= lax.broadcasted_iota(I32, (BLK, BLK), 1)

    heads = tuple(slice(h * dh, (h + 1) * dh) for h in range(NH))
    qs = [q_ref[0, :, hs] * jnp.asarray(dh ** -0.5, BF16) for hs in heads]

    def scores(j, h):
        c0 = pl.multiple_of(j * BLK, BLK)
        return _dot_nt(k_ref[0, pl.ds(c0, BLK), heads[h]], qs[h])

    def values(j, h, p):
        c0 = pl.multiple_of(j * BLK, BLK)
        return _dot(vt_ref[h, :, pl.ds(c0, BLK)], p)

    gates = []
    for h in range(NH):
        km_hi, km_lo = _split(kmean_ref[:, heads[h]])
        gates.append(_dot_nt(km_hi, qs[h]) + _dot_nt(km_lo, qs[h]))
    s_own = [scores(i, h) for h in range(NH)]
    s_first = [scores(0, h) for h in range(NH)]

    start = []
    for h in range(NH):
        gate = jnp.where(jf < i_f, gates[h], NEG)
        selm = jnp.zeros((nb, BLK), F32)
        for _ in range(top):
            m = jnp.max(gate, axis=0, keepdims=True)
            idx = jnp.min(jnp.where(gate == m, jf, float(nb)), axis=0, keepdims=True)
            hit = jf == idx
            selm = jnp.where(hit, jnp.where(m > 0.5 * NEG, 1.0, 0.0), selm)
            gate = jnp.where(hit, 2.0 * NEG, gate)
        sel_ref[h] = jnp.concatenate([selm, jnp.ones((SUBLANES, BLK), F32)], axis=0)
        sm = jnp.where(causal, s_own[h], NEG)
        m0 = jnp.max(sm, axis=0, keepdims=True)
        p = jnp.exp(sm - m0)
        s_ref[0, h] = s_first[h]
        p_ref[0, h] = p.astype(BF16)
        start += [m0, jnp.zeros((VT_ROWS, BLK), F32)]

    def prev_of(t):
        return jnp.where(t == 0, i, t - 1), jnp.where(t == 0, nb, t - 1)

    def stage(t, src, dst, carry):
        blk_prev, row_prev = prev_of(t)
        nxt = jnp.minimum(t + 1, nb - 1)
        for h in range(NH):
            s_ref[dst, h] = scores(nxt, h)
        pv = [values(blk_prev, h, p_ref[src, h]) for h in range(NH)]
        out = []
        for h in range(NH):
            m_i, acc = carry[2 * h:2 * h + 2]
            on = sel_ref[h, pl.ds(t, 1), :] > 0.5
            on_prev = sel_ref[h, pl.ds(row_prev, 1), :] > 0.5
            m_new = jnp.where(on, jnp.maximum(m_i, jnp.max(s_ref[src, h], axis=0, keepdims=True)), m_i)
            p_ref[dst, h] = jnp.exp(s_ref[src, h] - m_new).astype(BF16)
            out += [m_new, jnp.exp(m_i - m_new) * (acc + jnp.where(on_prev, pv[h], 0.0))]
        return tuple(out)

    n_pairs = (i + 1) // 2
    fin = lax.fori_loop(0, n_pairs, lambda u, c: stage(2 * u + 1, 1, 0, stage(2 * u, 0, 1, c)), tuple(start))
    blk_prev, row_prev = prev_of(2 * n_pairs)
    outs = []
    for h in range(NH):
        on_prev = sel_ref[h, pl.ds(row_prev, 1), :] > 0.5
        acc = fin[2 * h + 1] + jnp.where(on_prev, values(blk_prev, h, p_ref[0, h]), 0.0)
        outs.append(acc[0:dh] / acc[dh:dh + 1])
    o_ref[0] = jnp.concatenate(outs, axis=0).astype(o_ref.dtype)


def _moba_attention(q, k, v, seq):
    B, _, width = q.shape
    BLK, NH = MOBA_BLOCK, MOBA_HEADS_PER_STEP
    hw = NH * HEAD_DIM
    nb = seq // BLK
    top = min(MOBA_TOPK, max(nb - 1, 1))
    return pl.pallas_call(
        functools.partial(_moba_kernel, seq=seq, top=top),
        grid=(B, width // hw, nb),
        in_specs=[pl.BlockSpec((1, BLK, hw), lambda b, h, i: (b, i, h)),
                  pl.BlockSpec((1, seq, hw), lambda b, h, i: (b, 0, h)),
                  pl.BlockSpec((1, seq, hw), lambda b, h, i: (b, 0, h))],
        out_specs=pl.BlockSpec((1, hw, BLK), lambda b, h, i: (b, h, i)),
        out_shape=jax.ShapeDtypeStruct((B, width, seq), BF16),
        scratch_shapes=[pltpu.VMEM((nb, hw), F32), pltpu.VMEM((NH, VT_ROWS, seq), BF16),
                        pltpu.VMEM((NH, nb + SUBLANES, BLK), F32),
                        pltpu.VMEM((2, NH, BLK, BLK), F32), pltpu.VMEM((2, NH, BLK, BLK), BF16)],
        compiler_params=_cparams("parallel", "parallel", "arbitrary"), name="moba_attn",
    )(q, k, v)


def _route(logits):
    n_fine = MOE_GROUPS * MOE_EXPERTS_PER_GROUP
    lane = lax.broadcasted_iota(I32, logits.shape, 1)
    lf = lane.astype(F32)
    is_coarse = (lane >= n_fine) & (lane < n_fine + MOE_GROUPS)
    pc = _masked_softmax(logits, is_coarse)
    gv = jnp.max(pc, axis=-1, keepdims=True)
    gidx = jnp.min(jnp.where(is_coarse & (pc == gv), lf - float(n_fine), float(MOE_GROUPS)), axis=-1, keepdims=True)
    in_group = (lane < n_fine) & ((lane // MOE_EXPERTS_PER_GROUP).astype(F32) == gidx)
    pf = _masked_softmax(logits, in_group)
    cand = jnp.where(in_group, pf, -1.0)
    m1 = jnp.max(cand, axis=-1, keepdims=True)
    i1 = jnp.min(jnp.where(cand == m1, lf, float(LANES)), axis=-1, keepdims=True)
    cand = jnp.where(lf == i1, -1.0, cand)
    m2 = jnp.max(cand, axis=-1, keepdims=True)
    i2 = jnp.min(jnp.where(cand == m2, lf, float(LANES)), axis=-1, keepdims=True)
    tot = m1 + m2
    return jnp.where(lf == i1, gv * (m1 / tot), jnp.where(lf == i2, gv * (m2 / tot), 0.0))


def _moe_kernel(x_ref, wr_ref, br_ref, wg_ref, wu_ref, wd_ref, lg_ref, lb_ref, o_ref, gates_ref, acc_ref, *, alpha):
    E, FF = MOE_EXPERTS_PER_GROUP, MOE_FF
    g = pl.program_id(1)
    x = x_ref[...]

    @pl.when(g == 0)
    def _():
        gates_ref[...] = _route(_dot3(x, wr_ref[...]) + br_ref[...])
        acc_ref[...] = jnp.zeros_like(acc_ref)

    xb = x.astype(BF16)
    hg = _dot(xb, wg_ref[0])
    hu = _dot(xb, wu_ref[0])
    h = hg * _sigmoid(hg) * hu
    gates = gates_ref[...]
    lane = lax.broadcasted_iota(I32, gates.shape, 1)
    parts = []
    for e in range(E):
        ge = jnp.sum(jnp.where(lane == g * E + e, gates, 0.0), axis=-1, keepdims=True)
        parts.append((h[:, e * FF:(e + 1) * FF] * ge).astype(BF16))
    acc_ref[...] += _dot(jnp.concatenate(parts, axis=-1), wd_ref[0])

    @pl.when(g == pl.num_programs(1) - 1)
    def _():
        o_ref[...] = _layer_norm(alpha * x + acc_ref[...], lg_ref[...], lb_ref[...])


def _moe(x2d, wr, br, wg, wu, wd, ln_g, ln_b, alpha):
    T, D = x2d.shape
    tm = min(MOE_ROW_TILE, T)
    NG = wg.shape[0]
    whole = lambda a: pl.BlockSpec(a.shape, lambda i, g: (0, 0))
    per_g = lambda a: pl.BlockSpec((1,) + a.shape[1:], lambda i, g: (g, 0, 0))
    return pl.pallas_call(
        functools.partial(_moe_kernel, alpha=alpha),
        grid=(T // tm, NG),
        in_specs=[pl.BlockSpec((tm, D), lambda i, g: (i, 0)), whole(wr), whole(br), per_g(wg), per_g(wu), per_g(wd),
                  whole(ln_g), whole(ln_b)],
        out_specs=pl.BlockSpec((tm, D), lambda i, g: (i, 0)),
        out_shape=jax.ShapeDtypeStruct((T, D), F32),
        scratch_shapes=[pltpu.VMEM((tm, LANES), F32), pltpu.VMEM((tm, D), F32)],
        compiler_params=_cparams("parallel", "arbitrary"), name="moe",
    )(x2d, wr, br, wg, wu, wd, ln_g, ln_b)


def _rope_tables(seq):
    inv = 1.0 / (ROPE_THETA ** (jnp.arange(0, HEAD_DIM, 2, dtype=F32) / HEAD_DIM))
    ang = jnp.arange(seq, dtype=F32)[:, None] * inv[None, :]
    cos, sin = jnp.cos(ang), jnp.sin(ang)
    reps = LANES // HEAD_DIM
    return jnp.tile(jnp.concatenate([cos, cos], -1), (1, reps)), jnp.tile(jnp.concatenate([-sin, sin], -1), (1, reps))


def _selection_constants(seq):
    ncp = seq // CMP_STRIDE
    nbs = seq // SEL_BLOCK
    starts = jnp.arange(ncp) * CMP_STRIDE
    bstart = jnp.arange(nbs) * SEL_BLOCK
    ovl = jnp.clip(jnp.minimum(starts[:, None] + CMP_BLOCK, bstart[None, :] + SEL_BLOCK)
                   - jnp.maximum(starts[:, None], bstart[None, :]), 0, CMP_BLOCK)
    return (ovl.astype(F32) / CMP_BLOCK).astype(BF16).T


def _moe_weights(w_coarse, b_coarse, w_fine, b_fine, w_gate, w_up, w_down):
    D = w_coarse.shape[0]
    NG, E, _, FF = w_gate.shape
    n_fine = NG * E
    wr = jnp.zeros((D, LANES), F32)
    wr = wr.at[:, :n_fine].set(w_fine.transpose(1, 0, 2).reshape(D, n_fine)).at[:, n_fine:n_fine + NG].set(w_coarse)
    br = jnp.zeros((1, LANES), F32)
    br = br.at[0, :n_fine].set(b_fine.reshape(n_fine)).at[0, n_fine:n_fine + NG].set(b_coarse)
    wg = w_gate.transpose(0, 2, 1, 3).reshape(NG, D, E * FF).astype(BF16)
    wu = w_up.transpose(0, 2, 1, 3).reshape(NG, D, E * FF).astype(BF16)
    wd = w_down.reshape(NG, E * FF, D).astype(BF16)
    return wr, br, wg, wu, wd


def _even_mixer(x2d, B, S, cos2, sin2, w_in, pe_k, pe_v, k_w1, k_w2, v_w1, v_w2,
                lam_re, lam_im, log_step, b_re, b_im, c_re, c_im, d_skip, glu_w, glu_b, w_out, ln_g, ln_b, alpha):
    G, Z, dh = NSA_KV_GROUPS, NSA_HEADS_PER_GROUP, HEAD_DIM
    kvw = NSA_KV_WIDTH
    D = x2d.shape[1]
    cuts = [NSA_WIDTH + i * kvw for i in range(7)]
    w_q, w_kc, w_vc, w_ks, w_vs, w_kw, w_vw = [w_in[:, a:b] for a, b in zip([0] + cuts[:-1], cuts)]
    n_gate = 3 * NSA_HEADS
    w_g = jnp.pad(w_in[:, cuts[-1]:cuts[-1] + n_gate], ((0, 0), (0, LANES - n_gate)))
    w_u = w_in[:, cuts[-1] + n_gate:]
    weights = [w_q.astype(BF16), jnp.concatenate([w_kc, w_ks, w_kw], 1).astype(BF16),
               jnp.concatenate([w_vc, w_vs, w_vw], 1).astype(BF16), w_u.astype(BF16), w_g.astype(BF16)]
    q, kk, vv, u, gates = _project(x2d, cos2, sin2, weights,
                                   ["rope", "rope_heads", "plain_heads", "plain", "sigmoid"],
                                   [BF16, BF16, BF16, F32, F32], S)

    kc, ks, kw = [kk[i * G:(i + 1) * G].reshape(G, B, S, dh) for i in range(3)]
    vc, vs, vw = [vv[i * G:(i + 1) * G].reshape(G, B, S, dh) for i in range(3)]
    gates_t = gates[:, :n_gate].reshape(B, S, G, 3 * Z).transpose(0, 2, 3, 1)

    n_rows = S // CMP_STRIDE
    r = jnp.stack([kc, vc]).reshape(2, G * B, n_rows, CMP_STRIDE * dh)
    pe = jnp.stack([pe_k, pe_v]).reshape(2, 1, CMP_BLOCK * dh)
    w1 = jnp.stack([k_w1, v_w1]).astype(BF16)
    w2 = jnp.stack([k_w2, v_w2]).astype(BF16)
    cmp, cmp_t = _nsa_compress(r, pe, w1, w2, w2.transpose(0, 2, 1))
    kcm = cmp[0].reshape(G, B, n_rows, dh)
    vct = cmp_t[1].reshape(G, B, dh, n_rows)

    o_nsa = _nsa_attention(q.reshape(B, S, NSA_WIDTH), kcm, vct, ks, vs, kw, vw, gates_t, _selection_constants(S), S)

    C, L = S5_GROUP_CH, S5_CHUNK
    SG = u.shape[1] // C
    mt, pm, qt, al = _s5_operators(lam_re[:, None, :], lam_im[:, None, :], log_step[:, None, None],
                                   b_re.transpose(0, 2, 1), b_im.transpose(0, 2, 1), c_re, c_im)
    u_g = u.reshape(B, S // L, L, SG, C).transpose(3, 1, 0, 2, 4).reshape(SG, (S // L) * B, L * C)
    d_t = jnp.tile(d_skip.reshape(SG, 1, C), (1, 1, L))
    y = _s5_scan(u_g, mt, pm, qt, al, d_t, B)
    y = y.reshape(SG, S // L, B, L, C).transpose(2, 1, 3, 0, 4).reshape(B * S, SG * C)

    return _even_out(x2d, o_nsa.reshape(B * S, NSA_WIDTH), y, glu_w.astype(BF16), glu_b[None, :],
                     w_out[:NSA_WIDTH].astype(BF16), w_out[NSA_WIDTH:].astype(BF16), ln_g[None, :], ln_b[None, :], alpha)


def _odd_mixer(x2d, B, S, cos2, sin2, w_in, w_out, ln_g, ln_b, alpha):
    D = x2d.shape[1]
    q, k, v = _project(x2d, cos2, sin2, [w_in[:, i * D:(i + 1) * D].astype(BF16) for i in range(3)],
                       ["rope", "rope", "plain"], [BF16, BF16, BF16], S)
    attn_t = _moba_attention(q.reshape(B, S, D), k.reshape(B, S, D), v.reshape(B, S, D), S)
    return _odd_out(x2d, attn_t, w_out.astype(BF16), ln_g[None, :], ln_b[None, :], alpha)


def kernel(x, ev_w_in, nsa_pe_k, nsa_pe_v, nsa_cmp_k_w1, nsa_cmp_k_w2, nsa_cmp_v_w1, nsa_cmp_v_w2, s5_lambda_re, s5_lambda_im, s5_log_step, s5_b_re, s5_b_im, s5_c_re, s5_c_im, s5_d, s5_glu_w, s5_glu_b, ev_w_out, od_w_in, od_w_out, ln_mix_g, ln_mix_b, ln_ffn_g, ln_ffn_b, moe_w_coarse, moe_b_coarse, moe_w_fine, moe_b_fine, moe_w_gate, moe_w_up, moe_w_down):
    B, S, D = x.shape
    depth = ln_mix_g.shape[0]
    alpha = (2.0 * depth) ** 0.25
    cos2, sin2 = _rope_tables(S)
    h = x.reshape(B * S, D)
    for layer in range(depth):
        if layer % 2 == 0:
            e = layer // 2
            h = _even_mixer(h, B, S, cos2, sin2, ev_w_in[e], nsa_pe_k[e], nsa_pe_v[e], nsa_cmp_k_w1[e], nsa_cmp_k_w2[e],
                            nsa_cmp_v_w1[e], nsa_cmp_v_w2[e], s5_lambda_re[e], s5_lambda_im[e], s5_log_step[e],
                            s5_b_re[e], s5_b_im[e], s5_c_re[e], s5_c_im[e], s5_d[e], s5_glu_w[e], s5_glu_b[e],
                            ev_w_out[e], ln_mix_g[layer], ln_mix_b[layer], alpha)
        else:
            o = layer // 2
            h = _odd_mixer(h, B, S, cos2, sin2, od_w_in[o], od_w_out[o], ln_mix_g[layer], ln_mix_b[layer], alpha)
        wr, br, wg, wu, wd = _moe_weights(moe_w_coarse[layer], moe_b_coarse[layer], moe_w_fine[layer], moe_b_fine[layer],
                                          moe_w_gate[layer], moe_w_up[layer], moe_w_down[layer])
        h = _moe(h, wr, br, wg, wu, wd, ln_ffn_g[layer][None, :], ln_ffn_b[layer][None, :], alpha)
    return h.reshape(B, S, D)
```
